```python
import jax
import jax.numpy as jnp
from jax import lax
import numpy as np

D_MODEL = 2048
BATCH = 4
SEQ = 2048
DEPTH = 4
DEC_BATCH = 128
DEC_SEQ = 4
PAST_LEN = 16384
PAGE_SIZE = 128

HEAD_DIM = 64
N_HEADS = D_MODEL // HEAD_DIM
D_DECAY_LORA = 96
D_AAA_LORA = 96
D_GATE_LORA = 256
RWKV_COLS = 3 * D_MODEL + D_DECAY_LORA + D_AAA_LORA + D_GATE_LORA
GN_EPS = 64e-5
POOL_WINDOWS = (2, 4, 8, 16)
POOL_GROUPS = len(POOL_WINDOWS)
POOL_GROUP_DIM = D_MODEL // POOL_GROUPS
POOL_BUF = 15
IN_COLS = RWKV_COLS + 3 * D_MODEL
PEER_HEADS = 8
PEER_QDIM = 256
N_KEYS = 128
N_EXPERTS = N_KEYS * N_KEYS
PEER_TOPK = 16
PEER_BLOCK = 128
NORM_EPS = 1e-6

kernel_name = "rwkv7_pool_peer_hybrid_step"


def rmsnorm(x, g):
    xf = x.astype(jnp.float32)
    y = xf * lax.rsqrt(jnp.mean(xf * xf, axis=-1, keepdims=True) + NORM_EPS)
    return (y * g).astype(x.dtype)


def _heads(t):
    return t.reshape(t.shape[0], t.shape[1], N_HEADS, HEAD_DIM)


def rwkv7_branch(R, prev_row, S0, mu, w0, w_up, a0, a_up, g_up, k_k, k_a, r_k, ln_w, ln_b):
    f32 = jnp.float32
    B, T, _ = R.shape
    R_prev = jnp.concatenate([prev_row[:, None, :].astype(R.dtype), R[:, :-1]], axis=1)
    Rm = R + mu * (R_prev - R)
    o1, o2, o3 = D_MODEL, 2 * D_MODEL, 3 * D_MODEL
    o4 = o3 + D_DECAY_LORA
    o5 = o4 + D_AAA_LORA
    r, k, v = Rm[..., :o1], Rm[..., o1:o2], Rm[..., o2:o3]
    wd, ad, gd = Rm[..., o3:o4], Rm[..., o4:o5], Rm[..., o5:]
    w_log = -jax.nn.softplus(-(w0 + jnp.tanh(wd) @ w_up).astype(f32)) - 0.5
    decay = jnp.exp(-jnp.exp(w_log))
    a = jax.nn.sigmoid((a0 + ad @ a_up).astype(f32))
    g = jax.nn.sigmoid(gd) @ g_up
    kk = _heads((k * k_k).astype(f32))
    kk = kk / jnp.maximum(jnp.sqrt(jnp.sum(kk * kk, axis=-1, keepdims=True)), 1e-12)
    k_mod = k.astype(f32) * (1 + (a - 1) * k_a)
    r_h, k_h, v_h = _heads(r.astype(f32)), _heads(k_mod), _heads(v.astype(f32))
    a_h, w_h = _heads(a), _heads(decay)

    def step(S, inp):
        r_t, w_t, k_t, v_t, kk_t, a_t = inp
        sa = jnp.einsum('bhij,bhj->bhi', S, -kk_t)
        S = (S * w_t[:, :, None, :]
             + sa[..., None] * (kk_t * a_t)[:, :, None, :]
             + v_t[..., None] * k_t[:, :, None, :])
        return S, jnp.einsum('bhij,bhj->bhi', S, r_t)

    xs = tuple(jnp.moveaxis(t, 1, 0) for t in (r_h, w_h, k_h, v_h, kk, a_h))
    S_T, y = lax.scan(step, S0.astype(f32), xs)
    y = jnp.moveaxis(y, 0, 1)
    mean = jnp.mean(y, axis=-1, keepdims=True)
    var = jnp.mean(jnp.square(y - mean), axis=-1, keepdims=True)
    y = ((y - mean) * lax.rsqrt(var + GN_EPS)).reshape(B, T, D_MODEL) * ln_w + ln_b
    bonus = jnp.sum(r_h * k_h * r_k, axis=-1, keepdims=True) * v_h
    y = (y + bonus.reshape(B, T, D_MODEL)) * g
    return y.astype(R.dtype), S_T


def pool_branch(p, buf, pos0, pool_w, pool_scale):
    f32 = jnp.float32
    B, T, _ = p.shape
    P = buf.shape[1]
    ext = jnp.concatenate([buf.astype(p.dtype), p], axis=1).astype(f32)
    cs = jnp.concatenate([jnp.zeros((B, 1, D_MODEL), f32), jnp.cumsum(ext, axis=1)], axis=1)
    end = P + jnp.arange(T) + 1
    pos = pos0 + jnp.arange(T)
    outs = []
    for gi, w in enumerate(POOL_WINDOWS):
        start = jnp.maximum(end - w, 0)
        cnt = jnp.minimum(w, pos + 1).astype(f32)
        lo, hi = gi * POOL_GROUP_DIM, (gi + 1) * POOL_GROUP_DIM
        s = cs[:, end, lo:hi] - cs[:, start, lo:hi]
        outs.append(s / cnt[None, :, None])
    pooled = jnp.concatenate(outs, axis=-1)
    mixed = (pooled - ext[:, P:]).reshape(B, T, POOL_GROUPS, POOL_GROUP_DIM)
    y = jnp.einsum('btgc,gcd->btgd', mixed, pool_w).reshape(B, T, D_MODEL) * pool_scale
    return y.astype(p.dtype), ext[:, -POOL_BUF:]


def peer_ffn(z, w_q, sub_keys, peer_u, peer_v):
    B, T, _ = z.shape
    N = B * T
    nb = -(-N // PEER_BLOCK)
    flat = jnp.pad(z.reshape(N, D_MODEL), ((0, nb * PEER_BLOCK - N), (0, 0)))

    def block(zb):
        q = (zb @ w_q).reshape(-1, PEER_HEADS, PEER_QDIM)
        half = PEER_QDIM // 2
        s1 = jnp.einsum('thc,kc->thk', q[..., :half], sub_keys[0])
        s2 = jnp.einsum('thc,kc->thk', q[..., half:], sub_keys[1])
        t1, i1 = lax.top_k(s1, PEER_TOPK)
        t2, i2 = lax.top_k(s2, PEER_TOPK)
        cand = (t1[..., :, None] + t2[..., None, :]).reshape(-1, PEER_HEADS, PEER_TOPK * PEER_TOPK)
        cidx = (i1[..., :, None] * N_KEYS + i2[..., None, :]).reshape(-1, PEER_HEADS, PEER_TOPK * PEER_TOPK)
        top, sel = lax.top_k(cand, PEER_TOPK)
        eidx = jnp.take_along_axis(cidx, sel, axis=-1)
        gate = jax.nn.softmax(top.astype(jnp.float32), axis=-1)
        h = jax.nn.gelu(jnp.einsum('thkd,td->thk', peer_u[eidx], zb))
        return jnp.einsum('thk,thkd->td', (gate * h).astype(zb.dtype), peer_v[eidx])

    out = lax.map(block, flat.reshape(nb, PEER_BLOCK, D_MODEL))
    return out.reshape(nb * PEER_BLOCK, D_MODEL)[:N].reshape(B, T, D_MODEL)


def decoder_layer(x, c, S0, shift0, buf0, pos0, lp):
    (n1, n2, wm, bm, win, mu, w0, wup, a0, aup, gup, kk_, ka, rk, lnw, lnb,
     pw, ps, wout, wq, sk, pu, pv) = lp
    mod = (jax.nn.silu(c) @ wm + bm)[:, None, :]
    sh1, sc1, gt1, sh2, sc2, gt2 = jnp.split(mod, 6, axis=-1)
    z = rmsnorm(x, n1) * (1 + sc1) + sh1
    P = z @ win
    R = P[..., :RWKV_COLS]
    p = P[..., RWKV_COLS:RWKV_COLS + D_MODEL]
    gate_a = P[..., RWKV_COLS + D_MODEL:RWKV_COLS + 2 * D_MODEL]
    gate_b = P[..., RWKV_COLS + 2 * D_MODEL:]
    y_a, S_T = rwkv7_branch(R, shift0, S0, mu, w0, wup, a0, aup, gup, kk_, ka, rk, lnw, lnb)
    y_b, buf_T = pool_branch(p, buf0, pos0, pw, ps)
    merged = jax.nn.sigmoid(gate_a) * y_a + jax.nn.sigmoid(gate_b) * y_b
    x = x + gt1 * (merged @ wout)
    z2 = rmsnorm(x, n2) * (1 + sc2) + sh2
    x = x + gt2 * peer_ffn(z2, wq, sk, pu, pv)
    return x, S_T, R[:, -1], buf_T


def setup_inputs(seed: int = 0) -> dict:
    key = jax.random.key(seed)
    ks = iter(jax.random.split(key, 40))
    f32 = jnp.float32

    def nrm(shape, scale):
        return scale * jax.random.normal(next(ks), shape, f32)

    def unif(shape, lo, hi):
        return jax.random.uniform(next(ks), shape, f32, lo, hi)

    D = D_MODEL
    return {
        "x_prompt": nrm((BATCH, SEQ, D), 1.0),
        "x_sample": nrm((DEC_BATCH, DEC_SEQ, D), 1.0),
        "c_prompt": nrm((BATCH, D), 1.0),
        "c_sample": nrm((DEC_BATCH, D), 1.0),
        "state_wkv": nrm((DEPTH, DEC_BATCH, N_HEADS, HEAD_DIM, HEAD_DIM), 0.3),
        "state_shift": nrm((DEPTH, DEC_BATCH, RWKV_COLS), 1.0),
        "state_pool": nrm((DEPTH, DEC_BATCH, POOL_BUF, D), 1.0),
        "norm1_g": 1.0 + nrm((DEPTH, D), 0.1),
        "norm2_g": 1.0 + nrm((DEPTH, D), 0.1),
        "final_g": 1.0 + nrm((D,), 0.1),
        "w_mod": nrm((DEPTH, D, 6 * D), 0.5 * D ** -0.5),
        "b_mod": nrm((DEPTH, 6 * D), 0.02),
        "w_in": nrm((DEPTH, D, IN_COLS), D ** -0.5),
        "mu_shift": unif((DEPTH, RWKV_COLS), 0.0, 1.0),
        "w0": unif((DEPTH, D), -5.0, 0.0),
        "w_up": nrm((DEPTH, D_DECAY_LORA, D), 0.1 * D_DECAY_LORA ** -0.5),
        "a0": nrm((DEPTH, D), 0.1),
        "a_up": nrm((DEPTH, D_AAA_LORA, D), 0.1 * D_AAA_LORA ** -0.5),
        "g_up": nrm((DEPTH, D_GATE_LORA, D), D_GATE_LORA ** -0.5),
        "k_k": 0.85 + nrm((DEPTH, D), 0.05),
        "k_a": 1.0 + nrm((DEPTH, D), 0.05),
        "r_k": nrm((DEPTH, N_HEADS, HEAD_DIM), 0.1),
        "ln_w": 1.0 + nrm((DEPTH, D), 0.1),
        "ln_b": nrm((DEPTH, D), 0.02),
        "pool_w": nrm((DEPTH, POOL_GROUPS, POOL_GROUP_DIM, POOL_GROUP_DIM), POOL_GROUP_DIM ** -0.5),
        "pool_scale": 1.0 + nrm((DEPTH, D), 0.1),
        "w_out": nrm((DEPTH, D, D), D ** -0.5),
        "w_q": nrm((DEPTH, D, PEER_HEADS * PEER_QDIM), D ** -0.5),
        "sub_keys": nrm((DEPTH, 2, N_KEYS, PEER_QDIM // 2), (PEER_QDIM // 2) ** -0.5),
        "peer_u": nrm((DEPTH, N_EXPERTS, D), D ** -0.5),
        "peer_v": nrm((DEPTH, N_EXPERTS, D), 0.5),
    }


def reference(x_prompt, x_sample, c_prompt, c_sample, state_wkv, state_shift, state_pool,
              norm1_g, norm2_g, final_g, w_mod, b_mod, w_in, mu_shift, w0, w_up, a0, a_up, g_up,
              k_k, k_a, r_k, ln_w, ln_b, pool_w, pool_scale, w_out, w_q, sub_keys, peer_u, peer_v):
    xp, xs = x_prompt, x_sample
    wkv_p, shift_p, pool_p = [], [], []
    wkv_s, shift_s, pool_s = [], [], []
    for l in range(DEPTH):
        lp = (norm1_g[l], norm2_g[l], w_mod[l], b_mod[l], w_in[l], mu_shift[l], w0[l], w_up[l],
              a0[l], a_up[l], g_up[l], k_k[l], k_a[l], r_k[l], ln_w[l], ln_b[l], pool_w[l],
              pool_scale[l], w_out[l], w_q[l], sub_keys[l], peer_u[l], peer_v[l])
        xp, s_p, sh_p, b_p = decoder_layer(
            xp, c_prompt,
            jnp.zeros((BATCH, N_HEADS, HEAD_DIM, HEAD_DIM), jnp.float32),
            jnp.zeros((BATCH, RWKV_COLS), x_prompt.dtype),
            jnp.zeros((BATCH, 0, D_MODEL), x_prompt.dtype),
            0, lp)
        xs, s_s, sh_s, b_s = decoder_layer(
            xs, c_sample, state_wkv[l], state_shift[l], state_pool[l], PAST_LEN, lp)
        wkv_p.append(s_p); shift_p.append(sh_p); pool_p.append(b_p)
        wkv_s.append(s_s); shift_s.append(sh_s); pool_s.append(b_s)
    y_prompt = rmsnorm(xp, final_g)
    y_sample = rmsnorm(xs, final_g)
    return (y_prompt, y_sample,
            jnp.stack(wkv_p, axis=0), jnp.stack(shift_p, axis=0), jnp.stack(pool_p, axis=0),
            jnp.stack(wkv_s, axis=0), jnp.stack(shift_s, axis=0), jnp.stack(pool_s, axis=0))
```

```python
import functools

import jax
import jax.numpy as jnp
from jax import lax
from jax.experimental import pallas as pl
from jax.experimental.pallas import tpu as pltpu

F32 = jnp.float32
BF16 = jnp.bfloat16

HEAD_DIM = 64
LANES = 128
GN_EPS = 64e-5
NORM_EPS = 1e-6
PAST_LEN = 16384
POOL_WINDOWS = (2, 4, 8, 16)
POOL_BUF = 15
PEER_TOPK = 16
N_KEYS = 128
VMEM_LIMIT = 56 * 1024 * 1024


def _cp(*sem):
    return pltpu.CompilerParams(dimension_semantics=sem, vmem_limit_bytes=VMEM_LIMIT)


def _rup(n, m):
    return (n + m - 1) // m * m


def _dot(a, b):
    return jnp.dot(a, b, preferred_element_type=F32)


def _dot_nt(a, b):
    return lax.dot_general(a, b, (((1,), (1,)), ((), ())), preferred_element_type=F32)


def _mod_kernel(c_ref, w_ref, b_ref, o_ref):
    c = c_ref[...]
    a = (c * jax.nn.sigmoid(c)).astype(BF16)
    o_ref[...] = _dot(a, w_ref[...].astype(BF16)) + b_ref[...]


def _modulation(c_all, w_mod, b_mod):
    L, D, N = w_mod.shape
    Mc = c_all.shape[0]
    tn = 512
    return pl.pallas_call(
        _mod_kernel,
        grid=(L, N // tn),
        in_specs=[
            pl.BlockSpec((Mc, D), lambda l, j: (0, 0)),
            pl.BlockSpec((None, D, tn), lambda l, j: (l, 0, j)),
            pl.BlockSpec((None, 1, tn), lambda l, j: (l, 0, j)),
        ],
        out_specs=pl.BlockSpec((None, Mc, tn), lambda l, j: (l, 0, j)),
        out_shape=jax.ShapeDtypeStruct((L, Mc, N), F32),
        compiler_params=_cp("parallel", "parallel"),
        name="modulation",
    )(c_all, w_mod, b_mod.reshape(L, 1, N))


def _resnorm_kernel(*refs, has_res, has_mod):
    it = iter(refs)
    x_ref = next(it)
    if has_res:
        y_ref = next(it)
        gt_ref = next(it)
    g_ref = next(it)
    if has_mod:
        sc_ref = next(it)
        sh_ref = next(it)
    if has_res:
        xo_ref = next(it)
    z_ref = next(it)
    x = x_ref[...]
    if has_res:
        x = x + gt_ref[...] * y_ref[...]
        xo_ref[...] = x
    ms = jnp.mean(x * x, axis=-1, keepdims=True)
    n = x * lax.rsqrt(ms + NORM_EPS) * g_ref[...]
    if has_mod:
        n = n * (1.0 + sc_ref[...]) + sh_ref[...]
    z_ref[...] = n.astype(z_ref.dtype)


def _resnorm(x, g, *, res=None, mod=None, z_dtype=BF16, tm=256):
    M, D = x.shape
    tm = min(tm, M)
    row = pl.BlockSpec((tm, D), lambda i: (i, 0))

    def mspec(m3, piece, bpg):
        return pl.BlockSpec((None, m3.shape[1], D), lambda i: (i // bpg, 0, piece))

    args, specs = [x], [row]
    if res is not None:
        y, m3, piece, bpg = res
        args += [y, m3]
        specs += [row, mspec(m3, piece, bpg)]
    args.append(g.reshape(1, D))
    specs.append(pl.BlockSpec((1, D), lambda i: (0, 0)))
    if mod is not None:
        m3, scp, shp, bpg = mod
        args += [m3, m3]
        specs += [mspec(m3, scp, bpg), mspec(m3, shp, bpg)]
    out_shape, out_specs = [], []
    if res is not None:
        out_shape.append(jax.ShapeDtypeStruct((M, D), F32))
        out_specs.append(row)
    out_shape.append(jax.ShapeDtypeStruct((M, D), z_dtype))
    out_specs.append(row)
    outs = pl.pallas_call(
        functools.partial(_resnorm_kernel, has_res=res is not None, has_mod=mod is not None),
        grid=(M // tm,),
        in_specs=specs,
        out_specs=out_specs,
        out_shape=out_shape,
        compiler_params=_cp("parallel"),
        name="resnorm",
    )(*args)
    return outs if res is not None else (None, outs[0])


def _mm_kernel(a_ref, b_ref, o_ref):
    o_ref[...] = _dot(a_ref[...], b_ref[...]).astype(o_ref.dtype)


def _mm(a, b, *, tm=1024, tn=512, out_dtype=F32):
    M, K = a.shape
    N = b.shape[1]
    tm, tn = min(tm, M), min(tn, N)
    return pl.pallas_call(
        _mm_kernel,
        grid=(M // tm, N // tn),
        in_specs=[pl.BlockSpec((tm, K), lambda i, j: (i, 0)),
                  pl.BlockSpec((K, tn), lambda i, j: (0, j))],
        out_specs=pl.BlockSpec((tm, tn), lambda i, j: (i, j)),
        out_shape=jax.ShapeDtypeStruct((M, N), out_dtype),
        compiler_params=_cp("parallel", "parallel"),
        name="mm",
    )(a, b)


def _softplus(u):
    return jnp.maximum(u, 0.0) + jnp.log1p(jnp.exp(-jnp.abs(u)))


def _prep_kernel(*refs, seq_len, per_row_state, lw, la):
    (r_ref, k_ref, v_ref, l_ref, hr_ref, hk_ref, hv_ref, hl_ref) = refs[:8]
    n = 8
    if per_row_state:
        sr_ref, sk_ref, sv_ref, sl_ref = refs[n:n + 4]
        n += 4
    (mur_ref, muk_ref, muv_ref, mul_ref, w0_ref, a0_ref, wup_ref, aup_ref, gup_ref) = refs[n:n + 9]
    ro_ref, ko_ref, vo_ref, lwo_ref, ao_ref, go_ref = refs[n + 9:]
    tm = r_ref.shape[0]
    base = pl.program_id(0) * tm
    row = lax.broadcasted_iota(jnp.int32, (tm, 1), 0)
    pos = (base + row) % seq_len
    first = row == 0
    start = pos == 0

    def mixed(x_ref, h_ref, s_ref, mu_ref):
        x = x_ref[...]
        prev = jnp.where(first, h_ref[7:8, :], pltpu.roll(x, 1, 0))
        if per_row_state:
            prev = jnp.where(start, s_ref[...], prev)
        else:
            prev = jnp.where(start, 0.0, prev)
        return x + mu_ref[...] * (prev - x)

    sr = sk = sv = sl = None
    if per_row_state:
        sr, sk, sv, sl = sr_ref, sk_ref, sv_ref, sl_ref
    r = mixed(r_ref, hr_ref, sr, mur_ref)
    k = mixed(k_ref, hk_ref, sk, muk_ref)
    v = mixed(v_ref, hv_ref, sv, muv_ref)
    lo = mixed(l_ref, hl_ref, sl, mul_ref)
    wd = lo[:, :lw]
    ad = lo[:, lw:lw + la]
    gd = lo[:, lw + la:]
    wlin = w0_ref[...] + _dot(jnp.tanh(wd).astype(BF16), wup_ref[...])
    w_log = -_softplus(-wlin) - 0.5
    ro_ref[...] = r
    ko_ref[...] = k
    vo_ref[...] = v
    lwo_ref[...] = -jnp.exp(w_log)
    ao_ref[...] = jax.nn.sigmoid(a0_ref[...] + _dot(ad.astype(BF16), aup_ref[...]))
    go_ref[...] = _dot(jax.nn.sigmoid(gd).astype(BF16), gup_ref[...])


def _rwkv_prep(P, D, LB, lw, la, seq_len, state_rows, mu_rkv, mu_l, w0, a0, wup, aup, gup, tm=128):
    M = P.shape[0]
    tm = min(tm, M)
    lcol = 6 * D // LB
    per_row_state = state_rows is not None

    def cur(w, c):
        return pl.BlockSpec((tm, w), lambda i: (i, c))

    def halo(w, c):
        return pl.BlockSpec((8, w), lambda i: (jnp.maximum(i * (tm // 8) - 1, 0), c))

    args = [P] * 8
    specs = [cur(D, 0), cur(D, 1), cur(D, 2), cur(LB, lcol),
             halo(D, 0), halo(D, 1), halo(D, 2), halo(LB, lcol)]
    if per_row_state:
        s_rkv, s_l = state_rows
        args += [s_rkv, s_rkv, s_rkv, s_l]
        specs += [cur(D, 0), cur(D, 1), cur(D, 2), cur(LB, 0)]

    def const(shape, c=0):
        return pl.BlockSpec(shape, lambda i: (0, c))

    args += [mu_rkv, mu_rkv, mu_rkv, mu_l, w0, a0, wup, aup, gup]
    specs += [const((1, D), 0), const((1, D), 1), const((1, D), 2), const((1, LB)),
              const((1, D)), const((1, D)), const(wup.shape), const(aup.shape), const(gup.shape)]
    out = jax.ShapeDtypeStruct((M, D), F32)
    return pl.pallas_call(
        functools.partial(_prep_kernel, seq_len=seq_len, per_row_state=per_row_state, lw=lw, la=la),
        grid=(M // tm,),
        in_specs=specs,
        out_specs=[pl.BlockSpec((tm, D), lambda i: (i, 0))] * 6,
        out_shape=[out] * 6,
        compiler_params=_cp("parallel"),
        name="rwkv_prep",
    )(*args)


def _split3(x):
    h = x.astype(BF16)
    r1 = x - h.astype(F32)
    m = r1.astype(BF16)
    lo = (r1 - m.astype(F32)).astype(BF16)
    return h, m, lo


def _scan_kernel(r_ref, k_ref, v_ref, lw_ref, a_ref, g_ref, kk_ref, ka_ref, rk_ref, lnw_ref, lnb_ref,
                 s0_ref, y_ref, st_ref, *, C, npairs):
    c = pl.program_id(1)

    @pl.when(c == 0)
    def _():
        st_ref[...] = s0_ref[...]

    C2 = 2 * C
    lane = lax.broadcasted_iota(jnp.int32, (1, LANES), 1)
    m_a = lane < HEAD_DIM
    fa = m_a.astype(F32)
    fb = 1.0 - fa
    ri = lax.broadcasted_iota(jnp.int32, (C, C2), 0)
    cj = lax.broadcasted_iota(jnp.int32, (C, C2), 1)
    sj = jnp.where(cj >= C, cj - C, cj)
    strict = sj < ri
    incl = sj <= ri
    col_a = (cj < C).astype(F32)
    col_b = 1.0 - col_a
    tri = (lax.broadcasted_iota(jnp.int32, (C, C), 0) >= lax.broadcasted_iota(jnp.int32, (C, C), 1)).astype(BF16)
    bi = lax.broadcasted_iota(jnp.int32, (LANES, LANES), 0)
    bj = lax.broadcasted_iota(jnp.int32, (LANES, LANES), 1)
    bd = (bi < HEAD_DIM) == (bj < HEAD_DIM)
    eye = (lax.broadcasted_iota(jnp.int32, (C2, C2), 0) == lax.broadcasted_iota(jnp.int32, (C2, C2), 1)).astype(F32)
    eye_l = (bi == bj).astype(BF16)
    n_sq = max(C.bit_length() - 2, 0)

    def head_sum(x):
        sa = jnp.sum(x * fa, axis=1, keepdims=True)
        sb = jnp.sum(x * fb, axis=1, keepdims=True)
        return jnp.where(m_a, sa, sb)

    def stack(x):
        return jnp.concatenate([x * fa, x * fb], axis=0)

    def pair(p, carry):
        sl = pl.ds(pl.multiple_of(p * LANES, LANES), LANES)
        r = r_ref[:, sl]
        k = k_ref[:, sl]
        v = v_ref[:, sl]
        lw = lw_ref[:, sl]
        a = a_ref[:, sl]
        kk = k * kk_ref[:, sl]
        kk = kk / jnp.maximum(jnp.sqrt(head_sum(kk * kk)), 1e-12)
        km = k * (1.0 + (a - 1.0) * ka_ref[:, sl])
        h3, m3, l3 = _split3(lw)
        cum = _dot(tri, h3) + _dot(tri, m3) + _dot(tri, l3)
        w_in = jnp.exp(cum)
        w_inv = jnp.exp(-cum)
        rt = r * w_in
        at = -kk * jnp.exp(cum - lw)
        bt = kk * a * w_inv
        kt = km * w_inv
        rt16, at16 = rt.astype(BF16), at.astype(BF16)
        bst = stack(bt).astype(BF16)
        kst = stack(kt).astype(BF16)
        vst = stack(v).astype(BF16)
        a_ab = jnp.where(strict, _dot_nt(at16, bst), 0.0)
        a_ak = jnp.where(strict, _dot_nt(at16, kst), 0.0)
        a_rb = jnp.where(incl, _dot_nt(rt16, bst), 0.0)
        a_rk = jnp.where(incl, _dot_nt(rt16, kst), 0.0)
        s16 = st_ref[0, p].astype(BF16)
        rhs = _dot_nt(at16, s16) + _dot(a_ak.astype(BF16), vst)
        nbd = jnp.concatenate([a_ab * col_a, a_ab * col_b], axis=0)
        tinv = eye + nbd
        pw = nbd
        for _ in range(n_sq):
            pw16 = pw.astype(BF16)
            pw = _dot(pw16, pw16)
            tinv = tinv + _dot(tinv.astype(BF16), pw.astype(BF16))
        ust = _dot(tinv.astype(BF16), stack(rhs).astype(BF16))
        u = ust[:C] + ust[C:]
        y = (_dot_nt(rt16, s16) + _dot(a_rb.astype(BF16), ust.astype(BF16))
             + _dot(a_rk.astype(BF16), vst))
        uv = jnp.concatenate([u, v], axis=0).astype(BF16)
        bk = jnp.concatenate([bt, kt], axis=0).astype(BF16)
        uv_t = _dot_nt(eye_l, uv).astype(BF16)
        upd = _dot(uv_t, bk)
        s_new = (st_ref[0, p] + jnp.where(bd, upd, 0.0)) * w_in[C - 1:C, :]
        st_ref[0, p] = s_new
        mean = head_sum(y) * (1.0 / HEAD_DIM)
        d = y - mean
        var = head_sum(d * d) * (1.0 / HEAD_DIM)
        yn = d * lax.rsqrt(var + GN_EPS) * lnw_ref[:, sl] + lnb_ref[:, sl]
        bonus = head_sum(r * km * rk_ref[:, sl]) * v
        y_ref[:, sl] = (yn + bonus) * g_ref[:, sl]
        return carry

    lax.fori_loop(0, npairs, pair, 0)


def _rwkv_scan(r, k, v, lw, a, g, k_k, k_a, r_k, ln_w, ln_b, s0, B, T, C):
    M, D = r.shape
    npairs = D // LANES
    nc = T // C
    tok = pl.BlockSpec((C, D), lambda b, c: (b * nc + c, 0))
    par = pl.BlockSpec((1, D), lambda b, c: (0, 0))
    st = pl.BlockSpec((1, npairs, LANES, LANES), lambda b, c: (b, 0, 0, 0))
    return pl.pallas_call(
        functools.partial(_scan_kernel, C=C, npairs=npairs),
        grid=(B, nc),
        in_specs=[tok] * 6 + [par] * 5 + [st],
        out_specs=[tok, st],
        out_shape=[jax.ShapeDtypeStruct((M, D), F32), jax.ShapeDtypeStruct(s0.shape, F32)],
        compiler_params=_cp("parallel", "arbitrary"),
        name="rwkv_scan",
    )(r, k, v, lw, a, g, k_k, k_a, r_k, ln_w, ln_b, s0)


def _state_to_pairs(s):
    B, H = s.shape[:2]
    s = s.reshape(B, H // 2, 2, HEAD_DIM, HEAD_DIM)
    z = jnp.zeros_like(s[:, :, 0])
    top = jnp.concatenate([s[:, :, 0], z], axis=-1)
    bot = jnp.concatenate([z, s[:, :, 1]], axis=-1)
    return jnp.concatenate([top, bot], axis=-2)


def _pairs_to_state(sp):
    B, P = sp.shape[:2]
    a = sp[:, :, :HEAD_DIM, :HEAD_DIM]
    b = sp[:, :, HEAD_DIM:, HEAD_DIM:]
    return jnp.stack([a, b], axis=2).reshape(B, 2 * P, HEAD_DIM, HEAD_DIM)


def _pool_kernel(*refs, has_halo, seq_len, group_rows, out_row0, pos0, gd):
    if has_halo:
        p_ref, h_ref, pw_ref, ps_ref, o_ref = refs
    else:
        p_ref, pw_ref, ps_ref, o_ref = refs
    tm = p_ref.shape[0]
    if has_halo:
        base = (pl.program_id(0) * tm) % seq_len
        halo = jnp.where(base == 0, 0.0, h_ref[...])
        ext = jnp.concatenate([halo, p_ref[...]], axis=0)
        off = h_ref.shape[0]
        row = lax.broadcasted_iota(jnp.int32, (tm, 1), 0)
        pos = base + row + pos0
    else:
        ext = p_ref[...]
        off = 0
        row = lax.broadcasted_iota(jnp.int32, (tm, 1), 0)
        pos = pos0 + (row % group_rows) - out_row0
    for gi, w in enumerate(POOL_WINDOWS):
        x = ext[:, gi * gd:(gi + 1) * gd]
        s = x
        span = 1
        while span < w:
            s = s + pltpu.roll(s, span, 0)
            span *= 2
        cnt = jnp.minimum(w, pos + 1).astype(F32)
        mixed = s[off:] / cnt - x[off:]
        y = _dot(mixed.astype(BF16), pw_ref[gi])
        o_ref[:, gi * gd:(gi + 1) * gd] = y * ps_ref[:, gi * gd:(gi + 1) * gd]


def _pool(P, pcol, D, pw16, ps, *, seq_len, tm=256):
    M = P.shape[0]
    tm = min(tm, seq_len)
    G = len(POOL_WINDOWS)
    gd = D // G
    return pl.pallas_call(
        functools.partial(_pool_kernel, has_halo=True, seq_len=seq_len, group_rows=1, out_row0=0,
                          pos0=0, gd=gd),
        grid=(M // tm,),
        in_specs=[pl.BlockSpec((tm, D), lambda i: (i, pcol)),
                  pl.BlockSpec((16, D), lambda i: (jnp.maximum(i * (tm // 16) - 1, 0), pcol)),
                  pl.BlockSpec((G, gd, gd), lambda i: (0, 0, 0)),
                  pl.BlockSpec((1, D), lambda i: (0, 0))],
        out_specs=pl.BlockSpec((tm, D), lambda i: (i, 0)),
        out_shape=jax.ShapeDtypeStruct((M, D), F32),
        compiler_params=_cp("parallel"),
        name="pool",
    )(P, P, pw16, ps)


def _pool_ext(ext, D, pw16, ps, *, group_rows, out_row0, pos0, tm=384):
    M = ext.shape[0]
    tm = min(tm, M)
    G = len(POOL_WINDOWS)
    gd = D // G
    return pl.pallas_call(
        functools.partial(_pool_kernel, has_halo=False, seq_len=1, group_rows=group_rows,
                          out_row0=out_row0, pos0=pos0, gd=gd),
        grid=(M // tm,),
        in_specs=[pl.BlockSpec((tm, D), lambda i: (i, 0)),
                  pl.BlockSpec((G, gd, gd), lambda i: (0, 0, 0)),
                  pl.BlockSpec((1, D), lambda i: (0, 0))],
        out_specs=pl.BlockSpec((tm, D), lambda i: (i, 0)),
        out_shape=jax.ShapeDtypeStruct((M, D), F32),
        compiler_params=_cp("parallel"),
        name="pool_ext",
    )(ext, pw16, ps)


def _merge_out_kernel(ga_ref, gb_ref, ya_ref, yb_ref, w_ref, o_ref, m_scr):
    @pl.when(pl.program_id(1) == 0)
    def _():
        m = jax.nn.sigmoid(ga_ref[...]) * ya_ref[...] + jax.nn.sigmoid(gb_ref[...]) * yb_ref[...]
        m_scr[...] = m.astype(BF16)

    o_ref[...] = _dot(m_scr[...], w_ref[...])


def _merge_out(P, D, ya, yb, wout16, tm=512, tn=512):
    M = P.shape[0]
    tm, tn = min(tm, M), min(tn, D)
    tok = pl.BlockSpec((tm, D), lambda i, j: (i, 0))
    return pl.pallas_call(
        _merge_out_kernel,
        grid=(M // tm, D // tn),
        in_specs=[pl.BlockSpec((tm, D), lambda i, j: (i, 4)),
                  pl.BlockSpec((tm, D), lambda i, j: (i, 5)),
                  tok, tok,
                  pl.BlockSpec((D, tn), lambda i, j: (0, j))],
        out_specs=pl.BlockSpec((tm, tn), lambda i, j: (i, j)),
        out_shape=jax.ShapeDtypeStruct((M, D), F32),
        scratch_shapes=[pltpu.VMEM((tm, D), BF16)],
        compiler_params=_cp("parallel", "arbitrary"),
        name="merge_out",
    )(P, P, ya, yb, wout16)


def _peer_select_kernel(qt_ref, keys_ref, s1_ref, s2_ref, e1_ref, e2_ref, tau_ref,
                        t1_scr, t2_scr, cand_scr, top_scr, *, heads):
    half = N_KEYS

    def top16(x, scr):
        def body(i, x):
            m = jnp.max(x, axis=0, keepdims=True)
            scr[pl.ds(i, 1), :] = m
            return jnp.where(x == m, -jnp.inf, x)
        lax.fori_loop(0, PEER_TOPK, body, x)

    for h in range(heads):
        q1 = qt_ref[2 * h * half:(2 * h + 1) * half, :].astype(BF16)
        q2 = qt_ref[(2 * h + 1) * half:(2 * h + 2) * half, :].astype(BF16)
        s1 = _dot(keys_ref[0], q1)
        s2 = _dot(keys_ref[1], q2)
        top16(s1, t1_scr)
        top16(s2, t2_scr)
        t2 = t2_scr[...]
        for i in range(PEER_TOPK):
            cand_scr[i * PEER_TOPK:(i + 1) * PEER_TOPK, :] = t1_scr[i:i + 1, :] + t2
        top16(cand_scr[...], top_scr)
        top = top_scr[...]
        z = jnp.sum(jnp.exp(top - top[0:1]), axis=0, keepdims=True)
        s1_ref[h] = s1
        s2_ref[h] = s2
        e1_ref[h] = jnp.exp(s1 - t1_scr[0:1, :]) / z
        e2_ref[h] = jnp.exp(s2 - t2[0:1])
        tau_ref[h:h + 1, :] = top[PEER_TOPK - 1:PEER_TOPK]


def _peer_select(qt, keys16, heads, tl=128):
    QD, M = qt.shape
    tl = min(tl, M)
    big = jax.ShapeDtypeStruct((heads, N_KEYS, M), F32)
    bspec = pl.BlockSpec((heads, N_KEYS, tl), lambda i: (0, 0, i))
    return pl.pallas_call(
        functools.partial(_peer_select_kernel, heads=heads),
        grid=(M // tl,),
        in_specs=[pl.BlockSpec((QD, tl), lambda i: (0, i)),
                  pl.BlockSpec(keys16.shape, lambda i: (0, 0, 0))],
        out_specs=[bspec] * 4 + [pl.BlockSpec((heads, tl), lambda i: (0, i))],
        out_shape=[big] * 4 + [jax.ShapeDtypeStruct((heads, M), F32)],
        scratch_shapes=[pltpu.VMEM((PEER_TOPK, tl), F32), pltpu.VMEM((PEER_TOPK, tl), F32),
                        pltpu.VMEM((PEER_TOPK * PEER_TOPK, tl), F32), pltpu.VMEM((PEER_TOPK, tl), F32)],
        compiler_params=_cp("parallel"),
        name="peer_select",
    )(qt, keys16)


def _gelu_tanh(x):
    return 0.5 * x * (1.0 + jnp.tanh(0.7978845608028654 * (x + 0.044715 * (x * x * x))))


def _peer_dense_kernel(zt_ref, u_ref, vt_ref, s1_ref, s2_ref, e1_ref, e2_ref, tau_ref, o_ref,
                       ht_scr, wt_scr, *, ac, heads, rt):
    j = pl.program_id(1)

    @pl.when(j == 0)
    def _():
        o_ref[...] = jnp.zeros_like(o_ref)

    ht_scr[...] = _dot(u_ref[...], zt_ref[...])
    tiles = N_KEYS // rt

    def body(idx, carry):
        al = idx // tiles
        b0 = pl.multiple_of((idx % tiles) * rt, rt)
        a = j * ac + al
        acc = jnp.zeros((rt, zt_ref.shape[1]), F32)
        for h in range(heads):
            s1r = s1_ref[h, pl.ds(a, 1), :]
            e1r = e1_ref[h, pl.ds(a, 1), :]
            s2t = s2_ref[h, pl.ds(b0, rt), :]
            e2t = e2_ref[h, pl.ds(b0, rt), :]
            acc = acc + jnp.where((s1r + s2t) >= tau_ref[h:h + 1, :], e1r * e2t, 0.0)
        hrow = pl.multiple_of(al * N_KEYS + b0, rt)
        wt_scr[pl.ds(hrow, rt), :] = (acc * _gelu_tanh(ht_scr[pl.ds(hrow, rt), :])).astype(BF16)
        return carry

    lax.fori_loop(0, ac * tiles, body, 0)
    o_ref[...] += _dot(vt_ref[...], wt_scr[...])


def _peer_dense(zt, u16, vt16, s1, s2, e1, e2, tau, tm=512, ac=8):
    D, M = zt.shape
    E = u16.shape[0]
    heads = s1.shape[0]
    tm = min(tm, M)
    ec = ac * N_KEYS
    sel = pl.BlockSpec((heads, N_KEYS, tm), lambda i, j: (0, 0, i))
    return pl.pallas_call(
        functools.partial(_peer_dense_kernel, ac=ac, heads=heads, rt=16),
        grid=(M // tm, E // ec),
        in_specs=[pl.BlockSpec((D, tm), lambda i, j: (0, i)),
                  pl.BlockSpec((ec, D), lambda i, j: (j, 0)),
                  pl.BlockSpec((D, ec), lambda i, j: (0, j)),
                  sel, sel, sel, sel,
                  pl.BlockSpec((heads, tm), lambda i, j: (0, i))],
        out_specs=pl.BlockSpec((D, tm), lambda i, j: (0, i)),
        out_shape=jax.ShapeDtypeStruct((D, M), F32),
        scratch_shapes=[pltpu.VMEM((ec, tm), F32), pltpu.VMEM((ec, tm), BF16)],
        compiler_params=_cp("parallel", "arbitrary"),
        name="peer_dense",
    )(zt, u16, vt16, s1, s2, e1, e2, tau)


def _pad_cols(w, n):
    return jnp.pad(w, ((0, 0), (0, n - w.shape[1])))


def _pad_rows(w, n):
    return jnp.pad(w, ((0, n - w.shape[0]), (0, 0)))


def kernel(x_prompt, x_sample, c_prompt, c_sample, state_wkv, state_shift, state_pool, norm1_g, norm2_g, final_g, w_mod, b_mod, w_in, mu_shift, w0, w_up, a0, a_up, g_up, k_k, k_a, r_k, ln_w, ln_b, pool_w, pool_scale, w_out, w_q, sub_keys, peer_u, peer_v):
    Bp, Tp, D = x_prompt.shape
    Bs, Ts, _ = x_sample.shape
    L = w_in.shape[0]
    H = D // HEAD_DIM
    dw, da, dg = w_up.shape[1], a_up.shape[1], g_up.shape[1]
    LW, LA, LG = _rup(dw, LANES), _rup(da, LANES), _rup(dg, LANES)
    LB = LW + LA + LG
    o3 = 3 * D
    o4, o5, o6 = o3 + dw, o3 + dw + da, o3 + dw + da + dg
    peer_heads = w_q.shape[2] // (2 * N_KEYS)
    Mp, Ms = Bp * Tp, Bs * Ts
    TS8 = 8
    PG = 24
    assert Tp % 64 == 0 and Ts <= TS8 and POOL_BUF + Ts <= PG

    Mc = _rup(Bp + Bs, 8)
    c_all = jnp.pad(jnp.concatenate([c_prompt, c_sample], axis=0), ((0, Mc - Bp - Bs), (0, 0)))
    mod_all = _modulation(c_all, w_mod, b_mod)

    def repack_cols(t):
        return t[..., :o3], jnp.concatenate(
            [_pad_cols(t[..., o3:o4], LW), _pad_cols(t[..., o4:o5], LA), _pad_cols(t[..., o5:o6], LG)], axis=-1)

    xp = x_prompt.reshape(Mp, D)
    xs = x_sample.reshape(Ms, D)
    yp = ys = None
    outs = {k_: [] for k_ in ("wkv_p", "shift_p", "pool_p", "wkv_s", "shift_s", "pool_s")}
    zero_state = jnp.zeros((Bp, H // 2, LANES, LANES), F32)

    for l in range(L):
        w_rkv, w_l = repack_cols(w_in[l])
        winp = jnp.concatenate([w_rkv, w_in[l][:, o6:], w_l], axis=1).astype(BF16)
        mu_rkv, mu_l = repack_cols(mu_shift[l][None, :])
        wup = _pad_rows(w_up[l], LW).astype(BF16)
        aup = _pad_rows(a_up[l], LA).astype(BF16)
        gup = _pad_rows(g_up[l], LG).astype(BF16)
        row = lambda t: t[l].reshape(1, D)
        pw16 = pool_w[l].astype(BF16)
        wout16 = w_out[l].astype(BF16)
        wqt16 = w_q[l].T.astype(BF16)
        keys16 = sub_keys[l].astype(BF16)
        u16 = peer_u[l].astype(BF16)
        vt16 = peer_v[l].T.astype(BF16)
        mod_p = mod_all[l, :Bp].reshape(Bp, 1, 6 * D)
        tms = min(256, Ms)
        mod_s = jnp.repeat(mod_all[l, Bp:Bp + Bs], Ts, axis=0).reshape(Ms // tms, tms, 6 * D)

        def half_layer_1(x, y_prev, mod3, bpg, tmn):
            res = None if y_prev is None else (y_prev[0], y_prev[1], 5, bpg)
            x_new, z = _resnorm(x, norm1_g[l], res=res, mod=(mod3, 1, 0, bpg), tm=tmn)
            return (x if res is None else x_new), _mm(z, winp)

        def peer(z2):
            zt = z2.T
            qt = _mm(wqt16, zt, tm=512, tn=512)
            s1, s2, e1, e2, tau = _peer_select(qt, keys16, peer_heads)
            return _peer_dense(zt, u16, vt16, s1, s2, e1, e2, tau).T

        tmn = min(256, Tp)
        bpg = Tp // tmn
        xp, P = half_layer_1(xp, yp, mod_p, bpg, tmn)
        r, k, v, lw, a, g = _rwkv_prep(P, D, LB, LW, LA, Tp, None, mu_rkv, mu_l, row(w0), row(a0), wup, aup, gup)
        ya, s_fin = _rwkv_scan(r, k, v, lw, a, g, row(k_k), row(k_a), r_k[l].reshape(1, D), row(ln_w),
                               row(ln_b), zero_state, Bp, Tp, 64)
        yb = _pool(P, 3, D, pw16, row(pool_scale), seq_len=Tp)
        y1 = _merge_out(P, D, ya, yb, wout16)
        xp, z2 = _resnorm(xp, norm2_g[l], res=(y1, mod_p, 2, bpg), mod=(mod_p, 4, 3, bpg), tm=tmn)
        yp = (peer(z2), mod_p)
        P3 = P.reshape(Bp, Tp, -1)
        last = P3[:, -1]
        outs["wkv_p"].append(_pairs_to_state(s_fin))
        outs["shift_p"].append(jnp.concatenate(
            [last[:, :o3], last[:, 6 * D:6 * D + dw], last[:, 6 * D + LW:6 * D + LW + da],
             last[:, 6 * D + LW + LA:6 * D + LW + LA + dg]], axis=-1))
        outs["pool_p"].append(P3[:, Tp - POOL_BUF:, o3:o3 + D])

        xs, P = half_layer_1(xs, ys, mod_s, 1, tms)
        st_rkv, st_l = repack_cols(state_shift[l])
        expand = lambda t: jnp.pad(t[:, None, :], ((0, 0), (0, Ts - 1), (0, 0))).reshape(Ms, -1)
        r, k, v, lw, a, g = _rwkv_prep(P, D, LB, LW, LA, Ts, (expand(st_rkv), expand(st_l)), mu_rkv, mu_l,
                                       row(w0), row(a0), wup, aup, gup)
        pad8 = lambda t: jnp.pad(t.reshape(Bs, Ts, D), ((0, 0), (0, TS8 - Ts), (0, 0))).reshape(Bs * TS8, D)
        ya8, s_fin = _rwkv_scan(pad8(r), pad8(k), pad8(v), pad8(lw), pad8(a), pad8(g), row(k_k), row(k_a),
                                r_k[l].reshape(1, D), row(ln_w), row(ln_b), _state_to_pairs(state_wkv[l]),
                                Bs, TS8, TS8)
        ya = ya8.reshape(Bs, TS8, D)[:, :Ts].reshape(Ms, D)
        p_new = P[:, o3:o3 + D].reshape(Bs, Ts, D)
        ext = jnp.concatenate([state_pool[l], p_new], axis=1)
        extp = jnp.pad(ext, ((0, 0), (PG - POOL_BUF - Ts, 0), (0, 0))).reshape(Bs * PG, D)
        yb = _pool_ext(extp, D, pw16, row(pool_scale), group_rows=PG, out_row0=PG - Ts, pos0=PAST_LEN)
        yb = yb.reshape(Bs, PG, D)[:, PG - Ts:].reshape(Ms, D)
        y1 = _merge_out(P, D, ya, yb, wout16)
        xs, z2 = _resnorm(xs, norm2_g[l], res=(y1, mod_s, 2, 1), mod=(mod_s, 4, 3, 1), tm=tms)
        ys = (peer(z2), mod_s)
        last = P.reshape(Bs, Ts, -1)[:, -1]
        outs["wkv_s"].append(_pairs_to_state(s_fin))
        outs["shift_s"].append(jnp.concatenate(
            [last[:, :o3], last[:, 6 * D:6 * D + dw], last[:, 6 * D + LW:6 * D + LW + da],
             last[:, 6 * D + LW + LA:6 * D + LW + LA + dg]], axis=-1))
        outs["pool_s"].append(ext[:, -POOL_BUF:])

    tmn, tms = min(256, Tp), min(256, Ms)
    _, y_prompt = _resnorm(xp, final_g, res=(yp[0], yp[1], 5, Tp // tmn), z_dtype=F32, tm=tmn)
    _, y_sample = _resnorm(xs, final_g, res=(ys[0], ys[1], 5, 1), z_dtype=F32, tm=tms)
    st = lambda name: jnp.stack(outs[name], axis=0)
    return (y_prompt.reshape(Bp, Tp, D), y_sample.reshape(Bs, Ts, D),
            st("wkv_p"), st("shift_p"), st("pool_p"), st("wkv_s"), st("shift_s"), st("pool_s"))
```

```python
import functools

import jax
import jax.numpy as jnp
from jax import lax
from jax.experimental import pallas as pl
from jax.experimental.pallas import tpu as pltpu

F32 = jnp.float32
BF16 = jnp.bfloat16

HEAD_DIM = 64
LANES = 128
GN_EPS = 64e-5
NORM_EPS = 1e-6
PAST_LEN = 16384
POOL_WINDOWS = (2, 4, 8, 16)
POOL_BUF = 15
PEER_TOPK = 16
MXU_WIDTH = 256
PEER_LANE_TILE = 128
N_KEYS = 128
SCAN_ROWS = 64
VMEM_LIMIT = 56 * 1024 * 1024


def _cp(*sem):
    return pltpu.CompilerParams(dimension_semantics=sem, vmem_limit_bytes=VMEM_LIMIT)


def _rup(n, m):
    return (n + m - 1) // m * m


def _dot(a, b):
    return jnp.dot(a, b, preferred_element_type=F32)


def _dot_nt(a, b):
    return lax.dot_general(a, b, (((1,), (1,)), ((), ())), preferred_element_type=F32)


def _mod_kernel(c_ref, w_ref, b_ref, o_ref):
    c = c_ref[...]
    a = (c * jax.nn.sigmoid(c)).astype(BF16)
    o_ref[...] = _dot(a, w_ref[...].astype(BF16)) + b_ref[...]


def _modulation(c_all, w_mod, b_mod):
    L, D, N = w_mod.shape
    Mc = c_all.shape[0]
    tn = 512
    return pl.pallas_call(
        _mod_kernel,
        grid=(L, N // tn),
        in_specs=[
            pl.BlockSpec((Mc, D), lambda l, j: (0, 0)),
            pl.BlockSpec((None, D, tn), lambda l, j: (l, 0, j)),
            pl.BlockSpec((None, 1, tn), lambda l, j: (l, 0, j)),
        ],
        out_specs=pl.BlockSpec((None, Mc, tn), lambda l, j: (l, 0, j)),
        out_shape=jax.ShapeDtypeStruct((L, Mc, N), F32),
        compiler_params=_cp("parallel", "parallel"),
        name="modulation",
    )(c_all, w_mod, b_mod.reshape(L, 1, N))


def _resnorm_kernel(*refs, has_res, has_mod):
    it = iter(refs)
    x_ref = next(it)
    if has_res:
        y_ref = next(it)
        gt_ref = next(it)
    g_ref = next(it)
    if has_mod:
        sc_ref = next(it)
        sh_ref = next(it)
    if has_res:
        xo_ref = next(it)
    z_ref = next(it)
    x = x_ref[...]
    if has_res:
        x = x + gt_ref[...] * y_ref[...]
        xo_ref[...] = x
    ms = jnp.mean(x * x, axis=-1, keepdims=True)
    n = x * lax.rsqrt(ms + NORM_EPS) * g_ref[...]
    if has_mod:
        n = n * (1.0 + sc_ref[...]) + sh_ref[...]
    z_ref[...] = n.astype(z_ref.dtype)


def _resnorm(x, g, *, res=None, mod=None, z_dtype=BF16, tm=256):
    M, D = x.shape
    tm = min(tm, M)
    row = pl.BlockSpec((tm, D), lambda i: (i, 0))

    def mspec(m3, piece, bpg):
        return pl.BlockSpec((None, m3.shape[1], D), lambda i: (i // bpg, 0, piece))

    args, specs = [x], [row]
    if res is not None:
        y, m3, piece, bpg = res
        args += [y, m3]
        specs += [row, mspec(m3, piece, bpg)]
    args.append(g.reshape(1, D))
    specs.append(pl.BlockSpec((1, D), lambda i: (0, 0)))
    if mod is not None:
        m3, scp, shp, bpg = mod
        args += [m3, m3]
        specs += [mspec(m3, scp, bpg), mspec(m3, shp, bpg)]
    out_shape, out_specs = [], []
    if res is not None:
        out_shape.append(jax.ShapeDtypeStruct((M, D), F32))
        out_specs.append(row)
    out_shape.append(jax.ShapeDtypeStruct((M, D), z_dtype))
    out_specs.append(row)
    outs = pl.pallas_call(
        functools.partial(_resnorm_kernel, has_res=res is not None, has_mod=mod is not None),
        grid=(M // tm,),
        in_specs=specs,
        out_specs=out_specs,
        out_shape=out_shape,
        compiler_params=_cp("parallel"),
        name="resnorm",
    )(*args)
    return outs if res is not None else (None, outs[0])


def _mm_kernel(a_ref, b_ref, o_ref):
    o_ref[...] = _dot(a_ref[...], b_ref[...]).astype(o_ref.dtype)


def _mm(a, b, *, tm=1024, tn=512, out_dtype=F32):
    M, K = a.shape
    N = b.shape[1]
    tm, tn = min(tm, M), min(tn, N)
    return pl.pallas_call(
        _mm_kernel,
        grid=(M // tm, N // tn),
        in_specs=[pl.BlockSpec((tm, K), lambda i, j: (i, 0)),
                  pl.BlockSpec((K, tn), lambda i, j: (0, j))],
        out_specs=pl.BlockSpec((tm, tn), lambda i, j: (i, j)),
        out_shape=jax.ShapeDtypeStruct((M, N), out_dtype),
        compiler_params=_cp("parallel", "parallel"),
        name="mm",
    )(a, b)


def _softplus(u):
    return jnp.maximum(u, 0.0) + jnp.log1p(jnp.exp(-jnp.abs(u)))


def _prep_kernel(*refs, seq_len, per_row_state, lw, la):
    (r_ref, k_ref, v_ref, l_ref, hr_ref, hk_ref, hv_ref, hl_ref) = refs[:8]
    n = 8
    if per_row_state:
        sr_ref, sk_ref, sv_ref, sl_ref = refs[n:n + 4]
        n += 4
    (mur_ref, muk_ref, muv_ref, mul_ref, w0_ref, a0_ref, wup_ref, aup_ref, gup_ref) = refs[n:n + 9]
    ro_ref, ko_ref, vo_ref, lwo_ref, ao_ref, go_ref = refs[n + 9:]
    tm = r_ref.shape[0]
    base = pl.program_id(0) * tm
    row = lax.broadcasted_iota(jnp.int32, (tm, 1), 0)
    pos = (base + row) % seq_len
    first = row == 0
    start = pos == 0

    def mixed(x_ref, h_ref, s_ref, mu_ref):
        x = x_ref[...]
        prev = jnp.where(first, h_ref[7:8, :], pltpu.roll(x, 1, 0))
        if per_row_state:
            prev = jnp.where(start, s_ref[...], prev)
        else:
            prev = jnp.where(start, 0.0, prev)
        return x + mu_ref[...] * (prev - x)

    sr = sk = sv = sl = None
    if per_row_state:
        sr, sk, sv, sl = sr_ref, sk_ref, sv_ref, sl_ref
    r = mixed(r_ref, hr_ref, sr, mur_ref)
    k = mixed(k_ref, hk_ref, sk, muk_ref)
    v = mixed(v_ref, hv_ref, sv, muv_ref)
    lo = mixed(l_ref, hl_ref, sl, mul_ref)
    wd = lo[:, :lw]
    ad = lo[:, lw:lw + la]
    gd = lo[:, lw + la:]
    wlin = w0_ref[...] + _dot(jnp.tanh(wd).astype(BF16), wup_ref[...])
    w_log = -_softplus(-wlin) - 0.5
    ro_ref[...] = r
    ko_ref[...] = k
    vo_ref[...] = v
    lwo_ref[...] = -jnp.exp(w_log)
    ao_ref[...] = jax.nn.sigmoid(a0_ref[...] + _dot(ad.astype(BF16), aup_ref[...]))
    go_ref[...] = _dot(jax.nn.sigmoid(gd).astype(BF16), gup_ref[...])


def _rwkv_prep(P, D, LB, lw, la, seq_len, state_rows, mu_rkv, mu_l, w0, a0, wup, aup, gup, tm=128):
    M = P.shape[0]
    tm = min(tm, M)
    lcol = 6 * D // LB
    per_row_state = state_rows is not None

    def cur(w, c):
        return pl.BlockSpec((tm, w), lambda i: (i, c))

    def halo(w, c):
        return pl.BlockSpec((8, w), lambda i: (jnp.maximum(i * (tm // 8) - 1, 0), c))

    args = [P] * 8
    specs = [cur(D, 0), cur(D, 1), cur(D, 2), cur(LB, lcol),
             halo(D, 0), halo(D, 1), halo(D, 2), halo(LB, lcol)]
    if per_row_state:
        s_rkv, s_l = state_rows
        args += [s_rkv, s_rkv, s_rkv, s_l]
        specs += [cur(D, 0), cur(D, 1), cur(D, 2), cur(LB, 0)]

    def const(shape, c=0):
        return pl.BlockSpec(shape, lambda i: (0, c))

    args += [mu_rkv, mu_rkv, mu_rkv, mu_l, w0, a0, wup, aup, gup]
    specs += [const((1, D), 0), const((1, D), 1), const((1, D), 2), const((1, LB)),
              const((1, D)), const((1, D)), const(wup.shape), const(aup.shape), const(gup.shape)]
    out = jax.ShapeDtypeStruct((M, D), F32)
    return pl.pallas_call(
        functools.partial(_prep_kernel, seq_len=seq_len, per_row_state=per_row_state, lw=lw, la=la),
        grid=(M // tm,),
        in_specs=specs,
        out_specs=[pl.BlockSpec((tm, D), lambda i: (i, 0))] * 6,
        out_shape=[out] * 6,
        compiler_params=_cp("parallel"),
        name="rwkv_prep",
    )(*args)


def _split3(x):
    h = x.astype(BF16)
    r1 = x - h.astype(F32)
    m = r1.astype(BF16)
    lo = (r1 - m.astype(F32)).astype(BF16)
    return h, m, lo


def _scan_kernel(r_ref, k_ref, v_ref, lw_ref, a_ref, g_ref, kk_ref, ka_ref, rk_ref, lnw_ref, lnb_ref,
                 s0_ref, y_ref, st_ref, *, G, Tc):
    @pl.when(pl.program_id(2) == 0)
    def _():
        st_ref[...] = s0_ref[...]

    C = SCAN_ROWS
    nseq = C // Tc
    lane = lax.broadcasted_iota(jnp.int32, (1, LANES), 1)
    m_a = lane < HEAD_DIM
    fa = m_a.astype(F32)
    fb = 1.0 - fa
    ri = lax.broadcasted_iota(jnp.int32, (C, LANES), 0)
    cj = lax.broadcasted_iota(jnp.int32, (C, LANES), 1)
    sj = jnp.where(cj >= C, cj - C, cj)
    far = 1 << 20
    sj = jnp.where((ri // Tc) == (sj // Tc), sj, far)
    strict = sj < ri
    incl = sj <= ri
    col_a = (cj < C).astype(F32)
    col_b = 1.0 - col_a
    ti = lax.broadcasted_iota(jnp.int32, (C, 3 * C), 0)
    tj = lax.broadcasted_iota(jnp.int32, (C, 3 * C), 1) % C
    tri3 = jnp.where(jnp.where((ti // Tc) == (tj // Tc), tj, far) <= ti, 1.0, 0.0).astype(BF16)
    bi = lax.broadcasted_iota(jnp.int32, (LANES, LANES), 0)
    bj = lax.broadcasted_iota(jnp.int32, (LANES, LANES), 1)
    bd = (bi < HEAD_DIM) == (bj < HEAD_DIM)
    eye = (bi == bj).astype(F32)
    eye16 = eye.astype(BF16)
    levels = Tc.bit_length() - 1

    def head_sum(x):
        sa = jnp.sum(x * fa, axis=1, keepdims=True)
        sb = jnp.sum(x * fb, axis=1, keepdims=True)
        return jnp.where(m_a, sa, sb)

    def stack(x):
        return jnp.concatenate([x * fa, x * fb], axis=0)

    def seq_rows(x, s):
        return x[s * Tc:(s + 1) * Tc]

    Q = range(G)
    sls = [slice(q * LANES, (q + 1) * LANES) for q in Q]
    r = [r_ref[:, sl] for sl in sls]
    k = [k_ref[:, sl] for sl in sls]
    v = [v_ref[:, sl] for sl in sls]
    lw = [lw_ref[:, sl] for sl in sls]
    a = [a_ref[:, sl] for sl in sls]
    states = [[st_ref[s, q] for s in range(nseq)] for q in Q]
    cum = [_dot(tri3, jnp.concatenate(_split3(lw[q]), axis=0)) for q in Q]
    kk = [k[q] * kk_ref[:, sls[q]] for q in Q]
    kk = [kk[q] / jnp.maximum(jnp.sqrt(head_sum(kk[q] * kk[q])), 1e-12) for q in Q]
    km = [k[q] * (1.0 + (a[q] - 1.0) * ka_ref[:, sls[q]]) for q in Q]
    wcum = [jnp.exp(cum[q]) for q in Q]
    winv = [jnp.exp(-cum[q]) for q in Q]
    rt = [r[q] * wcum[q] for q in Q]
    at = [-kk[q] * jnp.exp(cum[q] - lw[q]) for q in Q]
    bt = [kk[q] * a[q] * winv[q] for q in Q]
    kt = [km[q] * winv[q] for q in Q]
    x16 = [jnp.concatenate([at[q], rt[q]], axis=0).astype(BF16) for q in Q]
    vst = [stack(v[q]).astype(BF16) for q in Q]
    amat = [_dot_nt(x16[q], jnp.concatenate([stack(bt[q]), stack(kt[q])], axis=0).astype(BF16)) for q in Q]
    if nseq == 1:
        xs = [_dot_nt(x16[q], states[q][0].astype(BF16)) for q in Q]
        xs_a = [xs[q][:C] for q in Q]
        xs_r = [xs[q][C:] for q in Q]
    else:
        parts = [[_dot_nt(jnp.concatenate([seq_rows(at[q], s), seq_rows(rt[q], s)], axis=0).astype(BF16),
                          states[q][s].astype(BF16)) for s in range(nseq)] for q in Q]
        xs_a = [jnp.concatenate([p_[:Tc] for p_ in parts[q]], axis=0) for q in Q]
        xs_r = [jnp.concatenate([p_[Tc:] for p_ in parts[q]], axis=0) for q in Q]
    a_ab = [jnp.where(strict, amat[q][:C, :LANES], 0.0) for q in Q]
    a_ak = [jnp.where(strict, amat[q][:C, LANES:], 0.0) for q in Q]
    a_rb = [jnp.where(incl, amat[q][C:, :LANES], 0.0) for q in Q]
    a_rk = [jnp.where(incl, amat[q][C:, LANES:], 0.0) for q in Q]
    rhs = [xs_a[q] + _dot(a_ak[q].astype(BF16), vst[q]) for q in Q]
    nbd = [jnp.concatenate([a_ab[q] * col_a, a_ab[q] * col_b], axis=0) for q in Q]
    tinv = [eye + nbd[q] for q in Q]
    if levels > 1:
        n16 = [nbd[q].astype(BF16) for q in Q]
        pw = [_dot(n16[q], n16[q]) for q in Q]
        for lev in range(1, levels):
            p16 = [pw[q].astype(BF16) for q in Q]
            if lev + 1 < levels:
                pt = [_dot(p16[q], jnp.concatenate([p16[q], tinv[q].astype(BF16)], axis=1)) for q in Q]
                pw = [pt[q][:, :LANES] for q in Q]
                tinv = [tinv[q] + pt[q][:, LANES:] for q in Q]
            else:
                tinv = [tinv[q] + _dot(p16[q], tinv[q].astype(BF16)) for q in Q]
    ust = [_dot(tinv[q].astype(BF16), stack(rhs[q]).astype(BF16)) for q in Q]
    u = [ust[q][:C] + ust[q][C:] for q in Q]
    y = [xs_r[q] + _dot(jnp.concatenate([a_rb[q], a_rk[q]], axis=1).astype(BF16),
                        jnp.concatenate([ust[q].astype(BF16), vst[q]], axis=0)) for q in Q]
    uvt = [[_dot_nt(eye16, jnp.concatenate([seq_rows(u[q], s), seq_rows(v[q], s)], axis=0).astype(BF16))
            for s in range(nseq)] for q in Q]
    upd = [[_dot(uvt[q][s].astype(BF16),
                 jnp.concatenate([seq_rows(bt[q], s), seq_rows(kt[q], s)], axis=0).astype(BF16))
            for s in range(nseq)] for q in Q]
    for q in Q:
        for s in range(nseq):
            st_ref[s, q] = ((states[q][s] + jnp.where(bd, upd[q][s], 0.0))
                            * wcum[q][(s + 1) * Tc - 1:(s + 1) * Tc, :])
    for q in Q:
        sl = sls[q]
        mean = head_sum(y[q]) * (1.0 / HEAD_DIM)
        d = y[q] - mean
        var = head_sum(d * d) * (1.0 / HEAD_DIM)
        yn = d * lax.rsqrt(var + GN_EPS) * lnw_ref[:, sl] + lnb_ref[:, sl]
        bonus = head_sum(r[q] * km[q] * rk_ref[:, sl]) * v[q]
        y_ref[:, sl] = (yn + bonus) * g_ref[:, sl]


def _rwkv_scan(r, k, v, lw, a, g, k_k, k_a, r_k, ln_w, ln_b, s0, Tseq, Tc, G=8):
    M, D = r.shape
    C = SCAN_ROWS
    nseq = C // Tc
    nc = Tseq // Tc
    assert 2 * C == LANES and (nc == 1 or nseq == 1)
    G = min(G, D // LANES)
    tok = pl.BlockSpec((C, G * LANES), lambda b, p, c: (b * nc + c, p))
    par = pl.BlockSpec((1, G * LANES), lambda b, p, c: (0, p))
    st = pl.BlockSpec((nseq, G, LANES, LANES), lambda b, p, c: (b, p, 0, 0))
    return pl.pallas_call(
        functools.partial(_scan_kernel, G=G, Tc=Tc),
        grid=(M // (C * nc), D // (G * LANES), nc),
        in_specs=[tok] * 6 + [par] * 5 + [st],
        out_specs=[tok, st],
        out_shape=[jax.ShapeDtypeStruct((M, D), F32), jax.ShapeDtypeStruct(s0.shape, F32)],
        compiler_params=_cp("parallel", "parallel", "arbitrary"),
        name="rwkv_scan",
    )(r, k, v, lw, a, g, k_k, k_a, r_k, ln_w, ln_b, s0)


def _state_to_pairs(s):
    B, H = s.shape[:2]
    s = s.reshape(B, H // 2, 2, HEAD_DIM, HEAD_DIM)
    z = jnp.zeros_like(s[:, :, 0])
    top = jnp.concatenate([s[:, :, 0], z], axis=-1)
    bot = jnp.concatenate([z, s[:, :, 1]], axis=-1)
    return jnp.concatenate([top, bot], axis=-2)


def _pairs_to_state(sp):
    B, P = sp.shape[:2]
    a = sp[:, :, :HEAD_DIM, :HEAD_DIM]
    b = sp[:, :, HEAD_DIM:, HEAD_DIM:]
    return jnp.stack([a, b], axis=2).reshape(B, 2 * P, HEAD_DIM, HEAD_DIM)


def _pool_kernel(*refs, has_halo, seq_len, group_rows, out_row0, pos0, gd):
    if has_halo:
        p_ref, h_ref, pw_ref, ps_ref, o_ref = refs
    else:
        p_ref, pw_ref, ps_ref, o_ref = refs
    tm = p_ref.shape[0]
    if has_halo:
        base = (pl.program_id(0) * tm) % seq_len
        halo = jnp.where(base == 0, 0.0, h_ref[...])
        ext = jnp.concatenate([halo, p_ref[...]], axis=0)
        off = h_ref.shape[0]
        row = lax.broadcasted_iota(jnp.int32, (tm, 1), 0)
        pos = base + row + pos0
    else:
        ext = p_ref[...]
        off = 0
        row = lax.broadcasted_iota(jnp.int32, (tm, 1), 0)
        pos = pos0 + (row % group_rows) - out_row0
    for gi, w in enumerate(POOL_WINDOWS):
        x = ext[:, gi * gd:(gi + 1) * gd]
        s = x
        span = 1
        while span < w:
            s = s + pltpu.roll(s, span, 0)
            span *= 2
        cnt = jnp.minimum(w, pos + 1).astype(F32)
        mixed = s[off:] / cnt - x[off:]
        y = _dot(mixed.astype(BF16), pw_ref[gi])
        o_ref[:, gi * gd:(gi + 1) * gd] = y * ps_ref[:, gi * gd:(gi + 1) * gd]


def _pool(P, pcol, D, pw16, ps, *, seq_len, tm=256):
    M = P.shape[0]
    tm = min(tm, seq_len)
    G = len(POOL_WINDOWS)
    gd = D // G
    return pl.pallas_call(
        functools.partial(_pool_kernel, has_halo=True, seq_len=seq_len, group_rows=1, out_row0=0,
                          pos0=0, gd=gd),
        grid=(M // tm,),
        in_specs=[pl.BlockSpec((tm, D), lambda i: (i, pcol)),
                  pl.BlockSpec((16, D), lambda i: (jnp.maximum(i * (tm // 16) - 1, 0), pcol)),
                  pl.BlockSpec((G, gd, gd), lambda i: (0, 0, 0)),
                  pl.BlockSpec((1, D), lambda i: (0, 0))],
        out_specs=pl.BlockSpec((tm, D), lambda i: (i, 0)),
        out_shape=jax.ShapeDtypeStruct((M, D), F32),
        compiler_params=_cp("parallel"),
        name="pool",
    )(P, P, pw16, ps)


def _pool_ext(ext, D, pw16, ps, *, group_rows, out_row0, pos0, tm=384):
    M = ext.shape[0]
    tm = min(tm, M)
    G = len(POOL_WINDOWS)
    gd = D // G
    return pl.pallas_call(
        functools.partial(_pool_kernel, has_halo=False, seq_len=1, group_rows=group_rows,
                          out_row0=out_row0, pos0=pos0, gd=gd),
        grid=(M // tm,),
        in_specs=[pl.BlockSpec((tm, D), lambda i: (i, 0)),
                  pl.BlockSpec((G, gd, gd), lambda i: (0, 0, 0)),
                  pl.BlockSpec((1, D), lambda i: (0, 0))],
        out_specs=pl.BlockSpec((tm, D), lambda i: (i, 0)),
        out_shape=jax.ShapeDtypeStruct((M, D), F32),
        compiler_params=_cp("parallel"),
        name="pool_ext",
    )(ext, pw16, ps)


def _merge_out_kernel(ga_ref, gb_ref, ya_ref, yb_ref, w_ref, o_ref, m_scr):
    @pl.when(pl.program_id(1) == 0)
    def _():
        m = jax.nn.sigmoid(ga_ref[...]) * ya_ref[...] + jax.nn.sigmoid(gb_ref[...]) * yb_ref[...]
        m_scr[...] = m.astype(BF16)

    o_ref[...] = _dot(m_scr[...], w_ref[...])


def _merge_out(P, D, ya, yb, wout16, tm=512, tn=512):
    M = P.shape[0]
    tm, tn = min(tm, M), min(tn, D)
    tok = pl.BlockSpec((tm, D), lambda i, j: (i, 0))
    return pl.pallas_call(
        _merge_out_kernel,
        grid=(M // tm, D // tn),
        in_specs=[pl.BlockSpec((tm, D), lambda i, j: (i, 4)),
                  pl.BlockSpec((tm, D), lambda i, j: (i, 5)),
                  tok, tok,
                  pl.BlockSpec((D, tn), lambda i, j: (0, j))],
        out_specs=pl.BlockSpec((tm, tn), lambda i, j: (i, j)),
        out_shape=jax.ShapeDtypeStruct((M, D), F32),
        scratch_shapes=[pltpu.VMEM((tm, D), BF16)],
        compiler_params=_cp("parallel", "arbitrary"),
        name="merge_out",
    )(P, P, ya, yb, wout16)


def _peer_select_kernel(qt_ref, keys_ref, s1_ref, s2_ref, e1_ref, e2_ref, tau_ref,
                        t1_scr, t2_scr, cand_scr, top_scr, *, heads):
    half = N_KEYS

    def top16(x, scr):
        def body(i, x):
            m = jnp.max(x, axis=0, keepdims=True)
            scr[pl.ds(i, 1), :] = m
            return jnp.where(x == m, -jnp.inf, x)
        lax.fori_loop(0, PEER_TOPK, body, x)

    for h in range(heads):
        q1 = qt_ref[2 * h * half:(2 * h + 1) * half, :].astype(BF16)
        q2 = qt_ref[(2 * h + 1) * half:(2 * h + 2) * half, :].astype(BF16)
        s1 = _dot(keys_ref[0], q1)
        s2 = _dot(keys_ref[1], q2)
        top16(s1, t1_scr)
        top16(s2, t2_scr)
        t2 = t2_scr[...]
        for i in range(PEER_TOPK):
            cand_scr[i * PEER_TOPK:(i + 1) * PEER_TOPK, :] = t1_scr[i:i + 1, :] + t2
        top16(cand_scr[...], top_scr)
        top = top_scr[...]
        z = jnp.sum(jnp.exp(top - top[0:1]), axis=0, keepdims=True)
        s1_ref[h] = s1
        s2_ref[h] = s2
        e1_ref[h] = jnp.exp(s1 - t1_scr[0:1, :]) / z
        e2_ref[h] = jnp.exp(s2 - t2[0:1])
        tau_ref[h:h + 1, :] = top[PEER_TOPK - 1:PEER_TOPK]


def _peer_select(qt, keys16, heads, tl=128):
    QD, M = qt.shape
    tl = min(tl, M)
    big = jax.ShapeDtypeStruct((heads, N_KEYS, M), F32)
    bspec = pl.BlockSpec((heads, N_KEYS, tl), lambda i: (0, 0, i))
    return pl.pallas_call(
        functools.partial(_peer_select_kernel, heads=heads),
        grid=(M // tl,),
        in_specs=[pl.BlockSpec((QD, tl), lambda i: (0, i)),
                  pl.BlockSpec(keys16.shape, lambda i: (0, 0, 0))],
        out_specs=[bspec] * 4 + [pl.BlockSpec((heads, tl), lambda i: (0, i))],
        out_shape=[big] * 4 + [jax.ShapeDtypeStruct((heads, M), F32)],
        scratch_shapes=[pltpu.VMEM((PEER_TOPK, tl), F32), pltpu.VMEM((PEER_TOPK, tl), F32),
                        pltpu.VMEM((PEER_TOPK * PEER_TOPK, tl), F32), pltpu.VMEM((PEER_TOPK, tl), F32)],
        compiler_params=_cp("parallel"),
        name="peer_select",
    )(qt, keys16)


def _gelu_tanh(x):
    return 0.5 * x * (1.0 + jnp.tanh(0.7978845608028654 * (x + 0.044715 * (x * x * x))))


def _peer_dense_kernel(zt_ref, u_ref, vt_ref, s1_ref, s2_ref, e1_ref, e2_ref, tau_ref, o_ref,
                       ht_scr, wt_scr, *, ac, heads, sub):
    j = pl.program_id(1)
    tm = zt_ref.shape[1]
    lt = min(PEER_LANE_TILE, tm)
    rows = sub * N_KEYS
    groups = N_KEYS // 8

    @pl.when(j == 0)
    def _():
        o_ref[...] = jnp.zeros_like(o_ref)

    mxu_n = min(MXU_WIDTH, tm)
    halves = tm // mxu_n
    units = [(al, t0) for t0 in range(0, tm, lt) for al in range(sub)]
    nu = len(units)
    d_out = o_ref.shape[0]
    ob = d_out // (nu // halves)

    def scores(c, hh):
        ts = slice(hh * mxu_n, (hh + 1) * mxu_n)
        ht_scr[c * rows:(c + 1) * rows, ts] = _dot(u_ref[c * rows:(c + 1) * rows, :], zt_ref[:, ts])

    def weights(c, unit):
        al, t0 = units[unit]
        al = c * sub + al
        a = j * ac + al
        ts = slice(t0, t0 + lt)
        accs = [jnp.zeros((8, lt), F32) for _ in range(groups)]
        for h in range(heads):
            s1b = jnp.broadcast_to(s1_ref[h, pl.ds(a, 1), :][:, ts], (8, lt))
            e1b = jnp.broadcast_to(e1_ref[h, pl.ds(a, 1), :][:, ts], (8, lt))
            taub = jnp.broadcast_to(tau_ref[h:h + 1, ts], (8, lt))
            for g in range(groups):
                sel = (s1b + s2_ref[h, g * 8:(g + 1) * 8, ts]) >= taub
                accs[g] = accs[g] + jnp.where(sel, e1b * e2_ref[h, g * 8:(g + 1) * 8, ts], 0.0)
        for g in range(0, groups, 2):
            rs = slice(al * N_KEYS + g * 8, al * N_KEYS + (g + 2) * 8)
            w = jnp.concatenate([accs[g], accs[g + 1]], axis=0) * _gelu_tanh(ht_scr[rs, ts])
            wt_scr[rs, ts] = w.astype(BF16)

    def combine(c, piece):
        hh, rb = piece % halves, piece // halves
        ts = slice(hh * mxu_n, (hh + 1) * mxu_n)
        os_ = slice(rb * ob, (rb + 1) * ob)
        o_ref[os_, ts] += _dot(vt_ref[os_, c * rows:(c + 1) * rows], wt_scr[c * rows:(c + 1) * rows, ts])

    nsub = ac // sub
    for hh in range(halves):
        scores(0, hh)
    for c in range(nsub):
        for unit in range(nu):
            if c + 1 < nsub and unit < halves:
                scores(c + 1, unit)
            if c > 0:
                combine(c - 1, unit)
            weights(c, unit)
    for piece in range(nu):
        combine(nsub - 1, piece)


def _peer_dense(zt, u16, vt16, s1, s2, e1, e2, tau, tm=512, ac=8, sub=2):
    D, M = zt.shape
    E = u16.shape[0]
    heads = s1.shape[0]
    tm = min(tm, M)
    ec = ac * N_KEYS
    sel = pl.BlockSpec((heads, N_KEYS, tm), lambda i, j: (0, 0, i))
    return pl.pallas_call(
        functools.partial(_peer_dense_kernel, ac=ac, heads=heads, sub=sub),
        grid=(M // tm, E // ec),
        in_specs=[pl.BlockSpec((D, tm), lambda i, j: (0, i)),
                  pl.BlockSpec((ec, D), lambda i, j: (j, 0)),
                  pl.BlockSpec((D, ec), lambda i, j: (0, j)),
                  sel, sel, sel, sel,
                  pl.BlockSpec((heads, tm), lambda i, j: (0, i))],
        out_specs=pl.BlockSpec((D, tm), lambda i, j: (0, i)),
        out_shape=jax.ShapeDtypeStruct((D, M), F32),
        scratch_shapes=[pltpu.VMEM((ec, tm), F32), pltpu.VMEM((ec, tm), BF16)],
        compiler_params=_cp("parallel", "arbitrary"),
        name="peer_dense",
    )(zt, u16, vt16, s1, s2, e1, e2, tau)


def _pad_cols(w, n):
    return jnp.pad(w, ((0, 0), (0, n - w.shape[1])))


def _pad_rows(w, n):
    return jnp.pad(w, ((0, n - w.shape[0]), (0, 0)))


def kernel(x_prompt, x_sample, c_prompt, c_sample, state_wkv, state_shift, state_pool, norm1_g, norm2_g, final_g, w_mod, b_mod, w_in, mu_shift, w0, w_up, a0, a_up, g_up, k_k, k_a, r_k, ln_w, ln_b, pool_w, pool_scale, w_out, w_q, sub_keys, peer_u, peer_v):
    Bp, Tp, D = x_prompt.shape
    Bs, Ts, _ = x_sample.shape
    L = w_in.shape[0]
    H = D // HEAD_DIM
    dw, da, dg = w_up.shape[1], a_up.shape[1], g_up.shape[1]
    LW, LA, LG = _rup(dw, LANES), _rup(da, LANES), _rup(dg, LANES)
    LB = LW + LA + LG
    o3 = 3 * D
    o4, o5, o6 = o3 + dw, o3 + dw + da, o3 + dw + da + dg
    peer_heads = w_q.shape[2] // (2 * N_KEYS)
    Mp, Ms = Bp * Tp, Bs * Ts
    TS8 = 8
    PG = 24
    assert Tp % 64 == 0 and Ts <= TS8 and POOL_BUF + Ts <= PG

    Mc = _rup(Bp + Bs, 8)
    c_all = jnp.pad(jnp.concatenate([c_prompt, c_sample], axis=0), ((0, Mc - Bp - Bs), (0, 0)))
    mod_all = _modulation(c_all, w_mod, b_mod)

    def repack_cols(t):
        return t[..., :o3], jnp.concatenate(
            [_pad_cols(t[..., o3:o4], LW), _pad_cols(t[..., o4:o5], LA), _pad_cols(t[..., o5:o6], LG)], axis=-1)

    xp = x_prompt.reshape(Mp, D)
    xs = x_sample.reshape(Ms, D)
    yp = ys = None
    outs = {k_: [] for k_ in ("wkv_p", "shift_p", "pool_p", "wkv_s", "shift_s", "pool_s")}
    zero_state = jnp.zeros((Bp, H // 2, LANES, LANES), F32)

    for l in range(L):
        w_rkv, w_l = repack_cols(w_in[l])
        winp = jnp.concatenate([w_rkv, w_in[l][:, o6:], w_l], axis=1).astype(BF16)
        mu_rkv, mu_l = repack_cols(mu_shift[l][None, :])
        wup = _pad_rows(w_up[l], LW).astype(BF16)
        aup = _pad_rows(a_up[l], LA).astype(BF16)
        gup = _pad_rows(g_up[l], LG).astype(BF16)
        row = lambda t: t[l].reshape(1, D)
        pw16 = pool_w[l].astype(BF16)
        wout16 = w_out[l].astype(BF16)
        wqt16 = w_q[l].T.astype(BF16)
        keys16 = sub_keys[l].astype(BF16)
        u16 = peer_u[l].astype(BF16)
        vt16 = peer_v[l].T.astype(BF16)
        mod_p = mod_all[l, :Bp].reshape(Bp, 1, 6 * D)
        tms = min(256, Ms)
        mod_s = jnp.repeat(mod_all[l, Bp:Bp + Bs], Ts, axis=0).reshape(Ms // tms, tms, 6 * D)

        def half_layer_1(x, y_prev, mod3, bpg, tmn):
            res = None if y_prev is None else (y_prev[0], y_prev[1], 5, bpg)
            x_new, z = _resnorm(x, norm1_g[l], res=res, mod=(mod3, 1, 0, bpg), tm=tmn)
            return (x if res is None else x_new), _mm(z, winp)

        def peer(z2):
            zt = z2.T
            qt = _mm(wqt16, zt, tm=512, tn=512)
            s1, s2, e1, e2, tau = _peer_select(qt, keys16, peer_heads)
            return _peer_dense(zt, u16, vt16, s1, s2, e1, e2, tau).T

        tmn = min(256, Tp)
        bpg = Tp // tmn
        xp, P = half_layer_1(xp, yp, mod_p, bpg, tmn)
        r, k, v, lw, a, g = _rwkv_prep(P, D, LB, LW, LA, Tp, None, mu_rkv, mu_l, row(w0), row(a0), wup, aup, gup)
        ya, s_fin = _rwkv_scan(r, k, v, lw, a, g, row(k_k), row(k_a), r_k[l].reshape(1, D), row(ln_w),
                               row(ln_b), zero_state, Tp, SCAN_ROWS)
        yb = _pool(P, 3, D, pw16, row(pool_scale), seq_len=Tp)
        y1 = _merge_out(P, D, ya, yb, wout16)
        xp, z2 = _resnorm(xp, norm2_g[l], res=(y1, mod_p, 2, bpg), mod=(mod_p, 4, 3, bpg), tm=tmn)
        yp = (peer(z2), mod_p)
        P3 = P.reshape(Bp, Tp, -1)
        last = P3[:, -1]
        outs["wkv_p"].append(_pairs_to_state(s_fin))
        outs["shift_p"].append(jnp.concatenate(
            [last[:, :o3], last[:, 6 * D:6 * D + dw], last[:, 6 * D + LW:6 * D + LW + da],
             last[:, 6 * D + LW + LA:6 * D + LW + LA + dg]], axis=-1))
        outs["pool_p"].append(P3[:, Tp - POOL_BUF:, o3:o3 + D])

        xs, P = half_layer_1(xs, ys, mod_s, 1, tms)
        st_rkv, st_l = repack_cols(state_shift[l])
        expand = lambda t: jnp.pad(t[:, None, :], ((0, 0), (0, Ts - 1), (0, 0))).reshape(Ms, -1)
        r, k, v, lw, a, g = _rwkv_prep(P, D, LB, LW, LA, Ts, (expand(st_rkv), expand(st_l)), mu_rkv, mu_l,
                                       row(w0), row(a0), wup, aup, gup)
        pad8 = lambda t: jnp.pad(t.reshape(Bs, Ts, D), ((0, 0), (0, TS8 - Ts), (0, 0))).reshape(Bs * TS8, D)
        ya8, s_fin = _rwkv_scan(pad8(r), pad8(k), pad8(v), pad8(lw), pad8(a), pad8(g), row(k_k), row(k_a),
                                r_k[l].reshape(1, D), row(ln_w), row(ln_b), _state_to_pairs(state_wkv[l]),
                                TS8, TS8)
        ya = ya8.reshape(Bs, TS8, D)[:, :Ts].reshape(Ms, D)
        p_new = P[:, o3:o3 + D].reshape(Bs, Ts, D)
        ext = jnp.concatenate([state_pool[l], p_new], axis=1)
        extp = jnp.pad(ext, ((0, 0), (PG - POOL_BUF - Ts, 0), (0, 0))).reshape(Bs * PG, D)
        yb = _pool_ext(extp, D, pw16, row(pool_scale), group_rows=PG, out_row0=PG - Ts, pos0=PAST_LEN)
        yb = yb.reshape(Bs, PG, D)[:, PG - Ts:].reshape(Ms, D)
        y1 = _merge_out(P, D, ya, yb, wout16)
        xs, z2 = _resnorm(xs, norm2_g[l], res=(y1, mod_s, 2, 1), mod=(mod_s, 4, 3, 1), tm=tms)
        ys = (peer(z2), mod_s)
        last = P.reshape(Bs, Ts, -1)[:, -1]
        outs["wkv_s"].append(_pairs_to_state(s_fin))
        outs["shift_s"].append(jnp.concatenate(
            [last[:, :o3], last[:, 6 * D:6 * D + dw], last[:, 6 * D + LW:6 * D + LW + da],
             last[:, 6 * D + LW + LA:6 * D + LW + LA + dg]], axis=-1))
        outs["pool_s"].append(ext[:, -POOL_BUF:])

    tmn, tms = min(256, Tp), min(256, Ms)
    _, y_prompt = _resnorm(xp, final_g, res=(yp[0], yp[1], 5, Tp // tmn), z_dtype=F32, tm=tmn)
    _, y_sample = _resnorm(xs, final_g, res=(ys[0], ys[1], 5, 1), z_dtype=F32, tm=tms)
    st = lambda name: jnp.stack(outs[name], axis=0)
    return (y_prompt.reshape(Bp, Tp, D), y_sample.reshape(Bs, Ts, D),
            st("wkv_p"), st("shift_p"), st("pool_p"), st("wkv_s"), st("shift_s"), st("pool_s"))
```

```python
import functools

import jax
import jax.numpy as jnp
from jax import lax
from jax.experimental import pallas as pl
from jax.experimental.pallas import tpu as pltpu

F32 = jnp.float32
BF16 = jnp.bfloat16

HEAD_DIM = 64
LANES = 128
MXU_WIDTH = 256
GN_EPS = 64e-5
NORM_EPS = 1e-6
PAST_LEN = 16384
POOL_WINDOWS = (2, 4, 8, 16)
POOL_BUF = 15
PEER_TOPK = 16
N_KEYS = 128
SCAN_ROWS = 64
VMEM_LIMIT = 56 * 1024 * 1024


def _cp(*sem):
    return pltpu.CompilerParams(dimension_semantics=sem, vmem_limit_bytes=VMEM_LIMIT)


def _rup(n, m):
    return (n + m - 1) // m * m


def _dot(a, b):
    return jnp.dot(a, b, preferred_element_type=F32)


def _dot_nt(a, b):
    return lax.dot_general(a, b, (((1,), (1,)), ((), ())), preferred_element_type=F32)


def _lspec(l, block, index_map):
    return pl.BlockSpec((None,) + tuple(block), lambda *g: (l,) + tuple(index_map(*g)))


def _mod_kernel(c_ref, w_ref, b_ref, o_ref):
    c = c_ref[...]
    a = (c * jax.nn.sigmoid(c)).astype(BF16)
    o_ref[...] = _dot(a, w_ref[...].astype(BF16)) + b_ref[...]


def _modulation(c_all, w_mod, b_mod):
    L, D, N = w_mod.shape
    Mc = c_all.shape[0]
    tn = 512
    return pl.pallas_call(
        _mod_kernel,
        grid=(L, N // tn),
        in_specs=[
            pl.BlockSpec((Mc, D), lambda l, j: (0, 0)),
            pl.BlockSpec((None, D, tn), lambda l, j: (l, 0, j)),
            pl.BlockSpec((None, 1, tn), lambda l, j: (l, 0, j)),
        ],
        out_specs=pl.BlockSpec((None, Mc, tn), lambda l, j: (l, 0, j)),
        out_shape=jax.ShapeDtypeStruct((L, Mc, N), F32),
        compiler_params=_cp("parallel", "parallel"),
        name="modulation",
    )(c_all, w_mod, b_mod.reshape(L, 1, N))


def _resnorm_kernel(*refs, has_res, has_mod):
    it = iter(refs)
    x_ref = next(it)
    if has_res:
        y_ref = next(it)
        gt_ref = next(it)
    g_ref = next(it)
    if has_mod:
        sc_ref = next(it)
        sh_ref = next(it)
    if has_res:
        xo_ref = next(it)
    z_ref = next(it)
    x = x_ref[...]
    if has_res:
        x = x + gt_ref[...] * y_ref[...]
        xo_ref[...] = x
    ms = jnp.mean(x * x, axis=-1, keepdims=True)
    n = x * lax.rsqrt(ms + NORM_EPS) * g_ref[...]
    if has_mod:
        n = n * (1.0 + sc_ref[...]) + sh_ref[...]
    z_ref[...] = n.astype(z_ref.dtype)


def _resnorm(x, g_spec, *, res=None, mod=None, z_dtype=BF16, tm=256):
    M, D = x.shape
    tm = min(tm, M)
    row = pl.BlockSpec((tm, D), lambda i: (i, 0))

    def mspec(m4, l, piece, bpg):
        return pl.BlockSpec((None, None, m4.shape[2], D), lambda i: (l, i // bpg, 0, piece))

    args, specs = [x], [row]
    if res is not None:
        y, m4, l, piece, bpg = res
        args += [y, m4]
        specs += [row, mspec(m4, l, piece, bpg)]
    args.append(g_spec[0])
    specs.append(g_spec[1])
    if mod is not None:
        m4, l, scp, shp, bpg = mod
        args += [m4, m4]
        specs += [mspec(m4, l, scp, bpg), mspec(m4, l, shp, bpg)]
    out_shape, out_specs = [], []
    if res is not None:
        out_shape.append(jax.ShapeDtypeStruct((M, D), F32))
        out_specs.append(row)
    out_shape.append(jax.ShapeDtypeStruct((M, D), z_dtype))
    out_specs.append(row)
    outs = pl.pallas_call(
        functools.partial(_resnorm_kernel, has_res=res is not None, has_mod=mod is not None),
        grid=(M // tm,),
        in_specs=specs,
        out_specs=out_specs,
        out_shape=out_shape,
        compiler_params=_cp("parallel"),
        name="resnorm",
    )(*args)
    return outs if res is not None else (None, outs[0])


def _mm_kernel(a_ref, b_ref, o_ref):
    o_ref[...] = _dot(a_ref[...], b_ref[...]).astype(o_ref.dtype)


def _mm(a, b, *, la=None, lb=None, tm=1024, tn=512, out_dtype=F32):
    M, K = a.shape[-2:]
    N = b.shape[-1]
    tm, tn = min(tm, M), min(tn, N)
    a_spec = (pl.BlockSpec((tm, K), lambda i, j: (i, 0)) if la is None
              else _lspec(la, (tm, K), lambda i, j: (i, 0)))
    b_spec = (pl.BlockSpec((K, tn), lambda i, j: (0, j)) if lb is None
              else _lspec(lb, (K, tn), lambda i, j: (0, j)))
    return pl.pallas_call(
        _mm_kernel,
        grid=(M // tm, N // tn),
        in_specs=[a_spec, b_spec],
        out_specs=pl.BlockSpec((tm, tn), lambda i, j: (i, j)),
        out_shape=jax.ShapeDtypeStruct((M, N), out_dtype),
        compiler_params=_cp("parallel", "parallel"),
        name="mm",
    )(a, b)


def _softplus(u):
    return jnp.maximum(u, 0.0) + jnp.log1p(jnp.exp(-jnp.abs(u)))


def _prep_kernel(*refs, seq_len, per_row_state, lw, la):
    (r_ref, k_ref, v_ref, l_ref, hr_ref, hk_ref, hv_ref, hl_ref) = refs[:8]
    n = 8
    if per_row_state:
        sr_ref, sk_ref, sv_ref, sl_ref = refs[n:n + 4]
        n += 4
    (mur_ref, muk_ref, muv_ref, mul_ref, w0_ref, a0_ref, wup_ref, aup_ref, gup_ref) = refs[n:n + 9]
    ro_ref, ko_ref, vo_ref, lwo_ref, ao_ref, go_ref = refs[n + 9:]
    tm = r_ref.shape[0]
    base = pl.program_id(0) * tm
    row = lax.broadcasted_iota(jnp.int32, (tm, 1), 0)
    pos = (base + row) % seq_len
    first = row == 0
    start = pos == 0

    def mixed(x_ref, h_ref, s_ref, mu_ref):
        x = x_ref[...]
        prev = jnp.where(first, h_ref[7:8, :], pltpu.roll(x, 1, 0))
        if per_row_state:
            prev = jnp.where(start, s_ref[...], prev)
        else:
            prev = jnp.where(start, 0.0, prev)
        return x + mu_ref[...] * (prev - x)

    sr = sk = sv = sl = None
    if per_row_state:
        sr, sk, sv, sl = sr_ref, sk_ref, sv_ref, sl_ref
    r = mixed(r_ref, hr_ref, sr, mur_ref)
    k = mixed(k_ref, hk_ref, sk, muk_ref)
    v = mixed(v_ref, hv_ref, sv, muv_ref)
    lo = mixed(l_ref, hl_ref, sl, mul_ref)
    wd = lo[:, :lw]
    ad = lo[:, lw:lw + la]
    gd = lo[:, lw + la:]
    wlin = w0_ref[...] + _dot(jnp.tanh(wd).astype(BF16), wup_ref[...])
    w_log = -_softplus(-wlin) - 0.5
    ro_ref[...] = r
    ko_ref[...] = k
    vo_ref[...] = v
    lwo_ref[...] = -jnp.exp(w_log)
    ao_ref[...] = jax.nn.sigmoid(a0_ref[...] + _dot(ad.astype(BF16), aup_ref[...]))
    go_ref[...] = _dot(jax.nn.sigmoid(gd).astype(BF16), gup_ref[...])


def _rwkv_prep(P, l, D, LB, lw, la, seq_len, state_rows, mu_rkv, mu_l, w0, a0, wup, aup, gup, tm=128):
    M = P.shape[0]
    tm = min(tm, M)
    lcol = 6 * D // LB
    per_row_state = state_rows is not None

    def cur(w, c):
        return pl.BlockSpec((tm, w), lambda i: (i, c))

    def halo(w, c):
        return pl.BlockSpec((8, w), lambda i: (jnp.maximum(i * (tm // 8) - 1, 0), c))

    args = [P] * 8
    specs = [cur(D, 0), cur(D, 1), cur(D, 2), cur(LB, lcol),
             halo(D, 0), halo(D, 1), halo(D, 2), halo(LB, lcol)]
    if per_row_state:
        s_rkv, s_l = state_rows
        args += [s_rkv, s_rkv, s_rkv, s_l]
        specs += [_lspec(l, (tm, D), lambda i, c=c: (i, c)) for c in range(3)]
        specs += [_lspec(l, (tm, LB), lambda i: (i, 0))]

    def const(arr, c=0, w=None):
        w = arr.shape[-1] if w is None else w
        return _lspec(l, (arr.shape[-2], w), lambda i: (0, c))

    args += [mu_rkv, mu_rkv, mu_rkv, mu_l, w0, a0, wup, aup, gup]
    specs += [const(mu_rkv, 0, D), const(mu_rkv, 1, D), const(mu_rkv, 2, D), const(mu_l),
              const(w0), const(a0), const(wup), const(aup), const(gup)]
    out = jax.ShapeDtypeStruct((M, D), F32)
    return pl.pallas_call(
        functools.partial(_prep_kernel, seq_len=seq_len, per_row_state=per_row_state, lw=lw, la=la),
        grid=(M // tm,),
        in_specs=specs,
        out_specs=[pl.BlockSpec((tm, D), lambda i: (i, 0))] * 6,
        out_shape=[out] * 6,
        compiler_params=_cp("parallel"),
        name="rwkv_prep",
    )(*args)


def _split3(x):
    h = x.astype(BF16)
    r1 = x - h.astype(F32)
    m = r1.astype(BF16)
    lo = (r1 - m.astype(F32)).astype(BF16)
    return h, m, lo


def _scan_kernel(*refs, G, Tc, zero_init):
    if zero_init:
        (r_ref, k_ref, v_ref, lw_ref, a_ref, g_ref, kk_ref, ka_ref, rk_ref, lnw_ref, lnb_ref,
         y_ref, so_ref, st_ref) = refs
    else:
        (r_ref, k_ref, v_ref, lw_ref, a_ref, g_ref, kk_ref, ka_ref, rk_ref, lnw_ref, lnb_ref,
         s0_ref, y_ref, so_ref, st_ref) = refs
    C = SCAN_ROWS
    nseq = C // Tc
    Q = range(G)

    @pl.when(pl.program_id(2) == 0)
    def _():
        if zero_init:
            st_ref[...] = jnp.zeros_like(st_ref)
        else:
            z = jnp.zeros((HEAD_DIM, HEAD_DIM), F32)
            for s in range(nseq):
                for q in Q:
                    top = jnp.concatenate([s0_ref[s, 2 * q], z], axis=1)
                    bot = jnp.concatenate([z, s0_ref[s, 2 * q + 1]], axis=1)
                    st_ref[s, q] = jnp.concatenate([top, bot], axis=0)

    lane = lax.broadcasted_iota(jnp.int32, (1, LANES), 1)
    m_a = lane < HEAD_DIM
    fa = m_a.astype(F32)
    fb = 1.0 - fa
    ri = lax.broadcasted_iota(jnp.int32, (C, LANES), 0)
    cj = lax.broadcasted_iota(jnp.int32, (C, LANES), 1)
    sj = jnp.where(cj >= C, cj - C, cj)
    far = 1 << 20
    sj = jnp.where((ri // Tc) == (sj // Tc), sj, far)
    strict = sj < ri
    incl = sj <= ri
    col_a = (cj < C).astype(F32)
    col_b = 1.0 - col_a
    ti = lax.broadcasted_iota(jnp.int32, (C, 3 * C), 0)
    tj = lax.broadcasted_iota(jnp.int32, (C, 3 * C), 1) % C
    tri3 = jnp.where(jnp.where((ti // Tc) == (tj // Tc), tj, far) <= ti, 1.0, 0.0).astype(BF16)
    bi = lax.broadcasted_iota(jnp.int32, (LANES, LANES), 0)
    bj = lax.broadcasted_iota(jnp.int32, (LANES, LANES), 1)
    bd = (bi < HEAD_DIM) == (bj < HEAD_DIM)
    eye = (bi == bj).astype(F32)
    eye16 = eye.astype(BF16)
    levels = Tc.bit_length() - 1

    def head_sum(x):
        sa = jnp.sum(x * fa, axis=1, keepdims=True)
        sb = jnp.sum(x * fb, axis=1, keepdims=True)
        return jnp.where(m_a, sa, sb)

    def stack(x):
        return jnp.concatenate([x * fa, x * fb], axis=0)

    def seq_rows(x, s):
        return x[s * Tc:(s + 1) * Tc]

    sls = [slice(q * LANES, (q + 1) * LANES) for q in Q]
    r = [r_ref[:, sl] for sl in sls]
    k = [k_ref[:, sl] for sl in sls]
    v = [v_ref[:, sl] for sl in sls]
    lw = [lw_ref[:, sl] for sl in sls]
    a = [a_ref[:, sl] for sl in sls]
    states = [[st_ref[s, q] for s in range(nseq)] for q in Q]
    cum = [_dot(tri3, jnp.concatenate(_split3(lw[q]), axis=0)) for q in Q]
    kk = [k[q] * kk_ref[:, sls[q]] for q in Q]
    kk = [kk[q] / jnp.maximum(jnp.sqrt(head_sum(kk[q] * kk[q])), 1e-12) for q in Q]
    km = [k[q] * (1.0 + (a[q] - 1.0) * ka_ref[:, sls[q]]) for q in Q]
    wcum = [jnp.exp(cum[q]) for q in Q]
    winv = [jnp.exp(-cum[q]) for q in Q]
    rt = [r[q] * wcum[q] for q in Q]
    at = [-kk[q] * jnp.exp(cum[q] - lw[q]) for q in Q]
    bt = [kk[q] * a[q] * winv[q] for q in Q]
    kt = [km[q] * winv[q] for q in Q]
    x16 = [jnp.concatenate([at[q], rt[q]], axis=0).astype(BF16) for q in Q]
    vst = [stack(v[q]).astype(BF16) for q in Q]
    amat = [_dot_nt(x16[q], jnp.concatenate([stack(bt[q]), stack(kt[q])], axis=0).astype(BF16)) for q in Q]
    if nseq == 1:
        xs = [_dot_nt(x16[q], states[q][0].astype(BF16)) for q in Q]
        xs_a = [xs[q][:C] for q in Q]
        xs_r = [xs[q][C:] for q in Q]
    else:
        parts = [[_dot_nt(jnp.concatenate([seq_rows(at[q], s), seq_rows(rt[q], s)], axis=0).astype(BF16),
                          states[q][s].astype(BF16)) for s in range(nseq)] for q in Q]
        xs_a = [jnp.concatenate([p_[:Tc] for p_ in parts[q]], axis=0) for q in Q]
        xs_r = [jnp.concatenate([p_[Tc:] for p_ in parts[q]], axis=0) for q in Q]
    a_ab = [jnp.where(strict, amat[q][:C, :LANES], 0.0) for q in Q]
    a_ak = [jnp.where(strict, amat[q][:C, LANES:], 0.0) for q in Q]
    a_rb = [jnp.where(incl, amat[q][C:, :LANES], 0.0) for q in Q]
    a_rk = [jnp.where(incl, amat[q][C:, LANES:], 0.0) for q in Q]
    rhs = [xs_a[q] + _dot(a_ak[q].astype(BF16), vst[q]) for q in Q]
    nbd = [jnp.concatenate([a_ab[q] * col_a, a_ab[q] * col_b], axis=0) for q in Q]
    tinv = [eye + nbd[q] for q in Q]
    if levels > 1:
        n16 = [nbd[q].astype(BF16) for q in Q]
        pw = [_dot(n16[q], n16[q]) for q in Q]
        for lev in range(1, levels):
            p16 = [pw[q].astype(BF16) for q in Q]
            if lev + 1 < levels:
                pt = [_dot(p16[q], jnp.concatenate([p16[q], tinv[q].astype(BF16)], axis=1)) for q in Q]
                pw = [pt[q][:, :LANES] for q in Q]
                tinv = [tinv[q] + pt[q][:, LANES:] for q in Q]
            else:
                tinv = [tinv[q] + _dot(p16[q], tinv[q].astype(BF16)) for q in Q]
    ust = [_dot(tinv[q].astype(BF16), stack(rhs[q]).astype(BF16)) for q in Q]
    u = [ust[q][:C] + ust[q][C:] for q in Q]
    y = [xs_r[q] + _dot(jnp.concatenate([a_rb[q], a_rk[q]], axis=1).astype(BF16),
                        jnp.concatenate([ust[q].astype(BF16), vst[q]], axis=0)) for q in Q]
    uvt = [[_dot_nt(eye16, jnp.concatenate([seq_rows(u[q], s), seq_rows(v[q], s)], axis=0).astype(BF16))
            for s in range(nseq)] for q in Q]
    upd = [[_dot(uvt[q][s].astype(BF16),
                 jnp.concatenate([seq_rows(bt[q], s), seq_rows(kt[q], s)], axis=0).astype(BF16))
            for s in range(nseq)] for q in Q]
    for q in Q:
        for s in range(nseq):
            st_ref[s, q] = ((states[q][s] + jnp.where(bd, upd[q][s], 0.0))
                            * wcum[q][(s + 1) * Tc - 1:(s + 1) * Tc, :])
    for q in Q:
        sl = sls[q]
        mean = head_sum(y[q]) * (1.0 / HEAD_DIM)
        d = y[q] - mean
        var = head_sum(d * d) * (1.0 / HEAD_DIM)
        yn = d * lax.rsqrt(var + GN_EPS) * lnw_ref[:, sl] + lnb_ref[:, sl]
        bonus = head_sum(r[q] * km[q] * rk_ref[:, sl]) * v[q]
        y_ref[:, sl] = (yn + bonus) * g_ref[:, sl]

    @pl.when(pl.program_id(2) == pl.num_programs(2) - 1)
    def _():
        for s in range(nseq):
            for q in Q:
                full = st_ref[s, q]
                so_ref[s, 2 * q] = full[:HEAD_DIM, :HEAD_DIM]
                so_ref[s, 2 * q + 1] = full[HEAD_DIM:, HEAD_DIM:]


def _rwkv_scan(l, r, k, v, lw, a, g, k_k, k_a, r_k, ln_w, ln_b, s0, nstates, Tseq, Tc, G=8):
    M, D = r.shape
    C = SCAN_ROWS
    nseq = C // Tc
    nc = Tseq // Tc
    assert 2 * C == LANES and (nc == 1 or nseq == 1)
    G = min(G, D // LANES)
    H = D // HEAD_DIM
    tok = pl.BlockSpec((C, G * LANES), lambda b, p, c: (b * nc + c, p))
    par = _lspec(l, (1, G * LANES), lambda b, p, c: (0, p))
    st_out = pl.BlockSpec((nseq, 2 * G, HEAD_DIM, HEAD_DIM), lambda b, p, c: (b, p, 0, 0))
    args = [r, k, v, lw, a, g, k_k, k_a, r_k, ln_w, ln_b]
    specs = [tok] * 6 + [par] * 5
    if s0 is not None:
        args.append(s0)
        specs.append(_lspec(l, (nseq, 2 * G, HEAD_DIM, HEAD_DIM), lambda b, p, c: (b, p, 0, 0)))
    return pl.pallas_call(
        functools.partial(_scan_kernel, G=G, Tc=Tc, zero_init=s0 is None),
        grid=(M // (C * nc), D // (G * LANES), nc),
        in_specs=specs,
        out_specs=[tok, st_out],
        out_shape=[jax.ShapeDtypeStruct((M, D), F32),
                   jax.ShapeDtypeStruct((nstates, H, HEAD_DIM, HEAD_DIM), F32)],
        scratch_shapes=[pltpu.VMEM((nseq, G, LANES, LANES), F32)],
        compiler_params=_cp("parallel", "parallel", "arbitrary"),
        name="rwkv_scan",
    )(*args)


def _pool_kernel(*refs, has_halo, seq_len, group_rows, out_row0, pos0, gd):
    if has_halo:
        p_ref, h_ref, pw_ref, ps_ref, o_ref = refs
    else:
        p_ref, pw_ref, ps_ref, o_ref = refs
    tm = p_ref.shape[0]
    if has_halo:
        base = (pl.program_id(0) * tm) % seq_len
        halo = jnp.where(base == 0, 0.0, h_ref[...])
        ext = jnp.concatenate([halo, p_ref[...]], axis=0)
        off = h_ref.shape[0]
        row = lax.broadcasted_iota(jnp.int32, (tm, 1), 0)
        pos = base + row + pos0
    else:
        ext = p_ref[...]
        off = 0
        row = lax.broadcasted_iota(jnp.int32, (tm, 1), 0)
        pos = pos0 + (row % group_rows) - out_row0
    for gi, w in enumerate(POOL_WINDOWS):
        x = ext[:, gi * gd:(gi + 1) * gd]
        s = x
        span = 1
        while span < w:
            s = s + pltpu.roll(s, span, 0)
            span *= 2
        cnt = jnp.minimum(w, pos + 1).astype(F32)
        mixed = s[off:] / cnt - x[off:]
        y = _dot(mixed.astype(BF16), pw_ref[gi])
        o_ref[:, gi * gd:(gi + 1) * gd] = y * ps_ref[:, gi * gd:(gi + 1) * gd]


def _pool(P, l, pcol, D, pw16, ps, *, seq_len, tm=256):
    M = P.shape[0]
    tm = min(tm, seq_len)
    G = len(POOL_WINDOWS)
    gd = D // G
    return pl.pallas_call(
        functools.partial(_pool_kernel, has_halo=True, seq_len=seq_len, group_rows=1, out_row0=0,
                          pos0=0, gd=gd),
        grid=(M // tm,),
        in_specs=[pl.BlockSpec((tm, D), lambda i: (i, pcol)),
                  pl.BlockSpec((16, D), lambda i: (jnp.maximum(i * (tm // 16) - 1, 0), pcol)),
                  _lspec(l, (G, gd, gd), lambda i: (0, 0, 0)),
                  _lspec(l, (1, D), lambda i: (0, 0))],
        out_specs=pl.BlockSpec((tm, D), lambda i: (i, 0)),
        out_shape=jax.ShapeDtypeStruct((M, D), F32),
        compiler_params=_cp("parallel"),
        name="pool",
    )(P, P, pw16, ps)


def _pool_ext(ext, l, D, pw16, ps, *, group_rows, out_row0, pos0, tm=384):
    M = ext.shape[0]
    tm = min(tm, M)
    G = len(POOL_WINDOWS)
    gd = D // G
    return pl.pallas_call(
        functools.partial(_pool_kernel, has_halo=False, seq_len=1, group_rows=group_rows,
                          out_row0=out_row0, pos0=pos0, gd=gd),
        grid=(M // tm,),
        in_specs=[pl.BlockSpec((tm, D), lambda i: (i, 0)),
                  _lspec(l, (G, gd, gd), lambda i: (0, 0, 0)),
                  _lspec(l, (1, D), lambda i: (0, 0))],
        out_specs=pl.BlockSpec((tm, D), lambda i: (i, 0)),
        out_shape=jax.ShapeDtypeStruct((M, D), F32),
        compiler_params=_cp("parallel"),
        name="pool_ext",
    )(ext, pw16, ps)


def _merge_out_kernel(ga_ref, gb_ref, ya_ref, yb_ref, w_ref, o_ref, m_scr):
    @pl.when(pl.program_id(1) == 0)
    def _():
        m = jax.nn.sigmoid(ga_ref[...]) * ya_ref[...] + jax.nn.sigmoid(gb_ref[...]) * yb_ref[...]
        m_scr[...] = m.astype(BF16)

    o_ref[...] = _dot(m_scr[...], w_ref[...])


def _merge_out(P, l, D, ya, yb, wout16, tm=512, tn=512):
    M = P.shape[0]
    tm, tn = min(tm, M), min(tn, D)
    tok = pl.BlockSpec((tm, D), lambda i, j: (i, 0))
    return pl.pallas_call(
        _merge_out_kernel,
        grid=(M // tm, D // tn),
        in_specs=[pl.BlockSpec((tm, D), lambda i, j: (i, 4)),
                  pl.BlockSpec((tm, D), lambda i, j: (i, 5)),
                  tok, tok,
                  _lspec(l, (D, tn), lambda i, j: (0, j))],
        out_specs=pl.BlockSpec((tm, tn), lambda i, j: (i, j)),
        out_shape=jax.ShapeDtypeStruct((M, D), F32),
        scratch_shapes=[pltpu.VMEM((tm, D), BF16)],
        compiler_params=_cp("parallel", "arbitrary"),
        name="merge_out",
    )(P, P, ya, yb, wout16)


PEER_RANKS = PEER_TOPK + 1
PEER_RANK_ROWS = _rup(PEER_RANKS, 8)
PEER_PAIR_COUNTS = tuple(min(PEER_RANKS, PEER_RANKS // (i + 1)) for i in range(PEER_RANKS))
PEER_CAND_ROWS = _rup(sum(PEER_PAIR_COUNTS), 8)


def _peer_select_kernel(qt_ref, keys_ref, ethr_ref, e1_ref, e2_ref,
                        s1_scr, s2_ref, t1_scr, t2_scr, cand_scr, top_scr, *, heads):
    half = N_KEYS
    tl = qt_ref.shape[1]
    neg = -jnp.inf
    for h in range(heads):
        q1 = qt_ref[2 * h * half:(2 * h + 1) * half, :].astype(BF16)
        q2 = qt_ref[(2 * h + 1) * half:(2 * h + 2) * half, :].astype(BF16)
        s1_scr[h] = _dot(keys_ref[0], q1)
        s2_ref[h] = _dot(keys_ref[1], q2)

    def next_max(x, prev):
        return jnp.max(jnp.where(x < prev, x, neg), axis=0, keepdims=True)

    def rank_body(i, prev):
        new = []
        for h in range(heads):
            m1 = next_max(s1_scr[h], prev[2 * h])
            m2 = next_max(s2_ref[h], prev[2 * h + 1])
            t1_scr[h, pl.ds(i, 1), :] = m1
            t2_scr[h, pl.ds(i, 1), :] = m2
            new += [m1, m2]
        return tuple(new)

    inf_row = jnp.full((1, tl), jnp.inf, F32)
    lax.fori_loop(0, PEER_RANKS, rank_body, (inf_row,) * (2 * heads))

    for h in range(heads):
        cand_scr[h, PEER_CAND_ROWS - 8:, :] = jnp.full((8, tl), neg, F32)
        r0 = 0
        for i, n in enumerate(PEER_PAIR_COUNTS):
            cand_scr[h, r0:r0 + n, :] = t1_scr[h, i:i + 1, :] + t2_scr[h, 0:n, :]
            r0 += n

    def cand_body(i, prev):
        new = []
        for h in range(heads):
            m = next_max(cand_scr[h], prev[h])
            top_scr[h, pl.ds(i, 1), :] = m
            new.append(m)
        return tuple(new)

    lax.fori_loop(0, PEER_RANKS, cand_body, (inf_row,) * heads)

    for h in range(heads):
        top = top_scr[h]
        t16 = top[PEER_TOPK - 1:PEER_TOPK]
        t17 = top[PEER_TOPK:PEER_TOPK + 1]
        mid = jnp.where(t17 > neg, 0.5 * (t16 + t17), t16)
        z = jnp.sum(jnp.exp(top[:PEER_TOPK] - top[0:1]), axis=0, keepdims=True)
        s1 = s1_scr[h]
        m2 = t2_scr[h, 0:1, :]
        ethr_ref[h] = jnp.exp((mid - m2) - s1)
        e1_ref[h] = jnp.exp(s1 - t1_scr[h, 0:1, :]) / z
        e2_ref[h] = jnp.exp(s2_ref[h] - m2)


def _peer_select(qt, keys16, l, heads, tl=128):
    QD, M = qt.shape
    tl = min(tl, M)
    big = jax.ShapeDtypeStruct((heads, N_KEYS, M), F32)
    bspec = pl.BlockSpec((heads, N_KEYS, tl), lambda i: (0, 0, i))
    ranks = pltpu.VMEM((heads, PEER_RANK_ROWS, tl), F32)
    return pl.pallas_call(
        functools.partial(_peer_select_kernel, heads=heads),
        grid=(M // tl,),
        in_specs=[pl.BlockSpec((QD, tl), lambda i: (0, i)),
                  _lspec(l, keys16.shape[1:], lambda i: (0, 0, 0))],
        out_specs=[bspec] * 3,
        out_shape=[big] * 3,
        scratch_shapes=[pltpu.VMEM((heads, N_KEYS, tl), F32), pltpu.VMEM((heads, N_KEYS, tl), F32), ranks, ranks,
                        pltpu.VMEM((heads, PEER_CAND_ROWS, tl), F32), ranks],
        compiler_params=_cp("parallel"),
        name="peer_select",
    )(qt, keys16)


PEER_LANE_TILE = 128
PEER_TICK_KEYS = 4
PEER_UNIT_KEYS = 2


def _gelu_tanh(x):
    return 0.5 * x * (1.0 + jnp.tanh(0.7978845608028654 * (x + 0.044715 * (x * x * x))))


def _peer_dense_kernel(zt_ref, u_ref, vt_ref, thr_a_ref, e1_a_ref, thr_b_ref, e1_b_ref, e2_ref,
                       o_ref, ht0, ht1, wt0, wt1, *, heads, nblk):
    s = pl.program_id(1)
    tm = zt_ref.shape[1]
    lt = min(PEER_LANE_TILE, tm)
    ka = PEER_TICK_KEYS
    rows = ka * N_KEYS
    groups = N_KEYS // 8

    @pl.when(s == 0)
    def _():
        o_ref[...] = jnp.zeros_like(o_ref)
        for buf in (ht0, ht1, wt0, wt1):
            buf[...] = jnp.zeros_like(buf)

    mxu_n = min(MXU_WIDTH, tm)
    n_lane = tm // mxu_n
    d_out = o_ref.shape[0]
    uk = PEER_UNIT_KEYS
    units = [(t0, ap) for t0 in range(0, tm, lt) for ap in range(0, ka, uk)]
    n_row = max(len(units) // n_lane, 1)
    ob = d_out // n_row

    def order_token(res):
        bits = pltpu.bitcast(res[:8, :lt], jnp.int32)
        bits = lax.shift_right_logical(lax.shift_right_logical(bits, 16), 16)
        return bits.astype(F32)

    def scores(half, ht, piece):
        ts = slice(piece * mxu_n, (piece + 1) * mxu_n)
        res = _dot(u_ref[half * rows:(half + 1) * rows, :], zt_ref[:, ts])
        ht[:, ts] = res
        return order_token(res)

    def combine(half, wt, piece):
        ts = slice((piece % n_lane) * mxu_n, (piece % n_lane + 1) * mxu_n)
        os_ = slice((piece // n_lane) * ob, (piece // n_lane + 1) * ob)
        res = _dot(vt_ref[os_, half * rows:(half + 1) * rows], wt[:, ts])
        o_ref[os_, ts] += res
        return order_token(res)

    def weights(thr_ref, e1_ref, k0, valid, ht, wt, unit, start):
        t0, ap = units[unit]
        ts = slice(t0, t0 + lt)
        accs = [[start for _ in range(groups)] for _ in range(uk)]
        for h in range(heads):
            thr_b = [jnp.broadcast_to(thr_ref[h, k0 + ap + d:k0 + ap + d + 1, ts], (8, lt)) for d in range(uk)]
            e1_b = [jnp.broadcast_to(e1_ref[h, k0 + ap + d:k0 + ap + d + 1, ts], (8, lt)) for d in range(uk)]
            for g in range(groups):
                e2t = e2_ref[h, g * 8:(g + 1) * 8, ts]
                for d in range(uk):
                    accs[d][g] = accs[d][g] + jnp.where(e2t >= thr_b[d], e1_b[d] * e2t, 0.0)
        for d in range(uk):
            for g in range(0, groups, 2):
                rs = slice((ap + d) * N_KEYS + g * 8, (ap + d) * N_KEYS + (g + 2) * 8)
                w = jnp.concatenate([accs[d][g], accs[d][g + 1]], axis=0) * _gelu_tanh(ht[rs, ts])
                wt[rs, ts] = jnp.where(valid, w, 0.0).astype(BF16)

    nu = len(units)
    n_comb = n_lane * n_row
    every = max(nu // n_lane, 1)
    zero = jnp.zeros((8, lt), F32)
    pending, ready = [], []

    def tile_start():
        nonlocal pending, ready
        start = zero
        for tok in ready:
            start = start + tok
        ready, pending = pending, []
        return start

    for unit in range(nu):
        if unit % every == 0 and unit // every < n_lane:
            pending.append(scores(0, ht0, unit // every))
        if unit < n_comb:
            pending.append(combine(0, wt0, unit))
        weights(thr_a_ref, e1_a_ref, ka, s >= 1, ht1, wt1, unit, tile_start())
    for unit in range(nu):
        if unit < n_comb:
            pending.append(combine(1, wt1, unit))
        if unit % every == 0 and unit // every < n_lane:
            pending.append(scores(1, ht1, unit // every))
        weights(thr_b_ref, e1_b_ref, 0, s < nblk, ht0, wt0, unit, tile_start())


def _peer_dense(zt, u16, vt16, l, ethr, e1, e2, tm=512):
    D, M = zt.shape
    E = u16.shape[1]
    heads = e2.shape[0]
    tm = min(tm, M)
    ka2 = 2 * PEER_TICK_KEYS
    ec = ka2 * N_KEYS
    nblk = E // ec
    cur = lambda s_: jnp.minimum(s_, nblk - 1)
    prv = lambda s_: jnp.clip(s_ - 1, 0, nblk - 1)
    full = pl.BlockSpec((heads, N_KEYS, tm), lambda i, s_: (0, 0, i))
    rows_prev = pl.BlockSpec((heads, ka2, tm), lambda i, s_: (0, prv(s_), i))
    rows_cur = pl.BlockSpec((heads, ka2, tm), lambda i, s_: (0, cur(s_), i))
    half_f32 = pltpu.VMEM((ec // 2, tm), F32)
    half_b16 = pltpu.VMEM((ec // 2, tm), BF16)
    return pl.pallas_call(
        functools.partial(_peer_dense_kernel, heads=heads, nblk=nblk),
        grid=(M // tm, nblk + 1),
        in_specs=[pl.BlockSpec((D, tm), lambda i, s_: (0, i)),
                  _lspec(l, (ec, D), lambda i, s_: (cur(s_), 0)),
                  _lspec(l, (D, ec), lambda i, s_: (0, prv(s_))),
                  rows_prev, rows_prev, rows_cur, rows_cur, full],
        out_specs=pl.BlockSpec((D, tm), lambda i, s_: (0, i)),
        out_shape=jax.ShapeDtypeStruct((D, M), F32),
        scratch_shapes=[half_f32, half_f32, half_b16, half_b16],
        compiler_params=_cp("parallel", "arbitrary"),
        name="peer_dense",
    )(zt, u16, vt16, ethr, e1, ethr, e1, e2)


def _pad_last(w, n):
    return jnp.pad(w, [(0, 0)] * (w.ndim - 1) + [(0, n - w.shape[-1])])


def _pad_rows(w, n):
    return jnp.pad(w, [(0, 0)] * (w.ndim - 2) + [(0, n - w.shape[-2]), (0, 0)])


def kernel(x_prompt, x_sample, c_prompt, c_sample, state_wkv, state_shift, state_pool, norm1_g, norm2_g, final_g, w_mod, b_mod, w_in, mu_shift, w0, w_up, a0, a_up, g_up, k_k, k_a, r_k, ln_w, ln_b, pool_w, pool_scale, w_out, w_q, sub_keys, peer_u, peer_v):
    Bp, Tp, D = x_prompt.shape
    Bs, Ts, _ = x_sample.shape
    L = w_in.shape[0]
    H = D // HEAD_DIM
    dw, da, dg = w_up.shape[1], a_up.shape[1], g_up.shape[1]
    LW, LA, LG = _rup(dw, LANES), _rup(da, LANES), _rup(dg, LANES)
    LB = LW + LA + LG
    o3 = 3 * D
    o4, o5, o6 = o3 + dw, o3 + dw + da, o3 + dw + da + dg
    peer_heads = w_q.shape[2] // (2 * N_KEYS)
    Mp, Ms = Bp * Tp, Bs * Ts
    TS8 = 8
    PG = 24
    assert Tp % SCAN_ROWS == 0 and Ts <= TS8 and POOL_BUF + Ts <= PG

    Mc = _rup(Bp + Bs, 8)
    c_all = jnp.pad(jnp.concatenate([c_prompt, c_sample], axis=0), ((0, Mc - Bp - Bs), (0, 0)))
    mod_all = _modulation(c_all, w_mod, b_mod)
    tmn, tms = min(256, Tp), min(256, Ms)
    bpg = Tp // tmn
    mod_p = mod_all[:, :Bp].reshape(L, Bp, 1, 6 * D)
    mod_s = jnp.repeat(mod_all[:, Bp:Bp + Bs], Ts, axis=1).reshape(L, Ms // tms, tms, 6 * D)

    def repack_cols(t):
        return t[..., :o3], jnp.concatenate(
            [_pad_last(t[..., o3:o4], LW), _pad_last(t[..., o4:o5], LA), _pad_last(t[..., o5:o6], LG)], axis=-1)

    w_rkv, w_l = repack_cols(w_in)
    winp = jnp.concatenate([w_rkv, w_in[..., o6:], w_l], axis=-1).astype(BF16)
    mu_rkv, mu_l = repack_cols(mu_shift[:, None, :])
    wup = _pad_rows(w_up, LW).astype(BF16)
    aup = _pad_rows(a_up, LA).astype(BF16)
    gup = _pad_rows(g_up, LG).astype(BF16)
    rows = lambda t: t.reshape(L, 1, D)
    w0r, a0r, kkr, kar, rkr, lnwr, lnbr, psr = (rows(t) for t in (w0, a0, k_k, k_a, r_k, ln_w, ln_b, pool_scale))
    n1r, n2r = rows(norm1_g), rows(norm2_g)
    pw16 = pool_w.astype(BF16)
    wout16 = w_out.astype(BF16)
    wqt16 = jnp.swapaxes(w_q, 1, 2).astype(BF16)
    keys16 = sub_keys.astype(BF16)
    u16 = peer_u.astype(BF16)
    vt16 = jnp.swapaxes(peer_v, 1, 2).astype(BF16)
    gain = lambda arr, l: (arr, _lspec(l, (1, D), lambda i: (0, 0)))
    st_rkv, st_l = repack_cols(state_shift)
    expand = lambda t: jnp.pad(t[:, :, None, :], ((0, 0), (0, 0), (0, Ts - 1), (0, 0))).reshape(L, Ms, -1)
    st_rkv, st_l = expand(st_rkv), expand(st_l)

    def shift_out(P, nseq, T):
        last = P.reshape(nseq, T, -1)[:, -1]
        return jnp.concatenate(
            [last[:, :o3], last[:, 6 * D:6 * D + dw], last[:, 6 * D + LW:6 * D + LW + da],
             last[:, 6 * D + LW + LA:6 * D + LW + LA + dg]], axis=-1)

    xp = x_prompt.reshape(Mp, D)
    xs = x_sample.reshape(Ms, D)
    yp = ys = None
    outs = {k_: [] for k_ in ("wkv_p", "shift_p", "pool_p", "wkv_s", "shift_s", "pool_s")}

    for l in range(L):
        def half_layer_1(x, y_prev, mod4, bpg_, tm_):
            res = None if y_prev is None else (y_prev, mod4, l - 1, 5, bpg_)
            x_new, z = _resnorm(x, gain(n1r, l), res=res, mod=(mod4, l, 1, 0, bpg_), tm=tm_)
            return (x if res is None else x_new), _mm(z, winp, lb=l)

        def peer(z2):
            zt = z2.T
            qt = _mm(wqt16, zt, la=l, tm=512, tn=512)
            ethr, e1, e2 = _peer_select(qt, keys16, l, peer_heads)
            return _peer_dense(zt, u16, vt16, l, ethr, e1, e2).T

        scan_par = (kkr, kar, rkr, lnwr, lnbr)

        xp, P = half_layer_1(xp, yp, mod_p, bpg, tmn)
        r, k, v, lw, a, g = _rwkv_prep(P, l, D, LB, LW, LA, Tp, None, mu_rkv, mu_l, w0r, a0r, wup, aup, gup)
        ya, s_fin = _rwkv_scan(l, r, k, v, lw, a, g, *scan_par, None, Bp, Tp, SCAN_ROWS)
        yb = _pool(P, l, 3, D, pw16, psr, seq_len=Tp)
        y1 = _merge_out(P, l, D, ya, yb, wout16)
        xp, z2 = _resnorm(xp, gain(n2r, l), res=(y1, mod_p, l, 2, bpg), mod=(mod_p, l, 4, 3, bpg), tm=tmn)
        yp = peer(z2)
        outs["wkv_p"].append(s_fin)
        outs["shift_p"].append(shift_out(P, Bp, Tp))
        outs["pool_p"].append(P.reshape(Bp, Tp, -1)[:, Tp - POOL_BUF:, o3:o3 + D])

        xs, P = half_layer_1(xs, ys, mod_s, 1, tms)
        r, k, v, lw, a, g = _rwkv_prep(P, l, D, LB, LW, LA, Ts, (st_rkv, st_l), mu_rkv, mu_l,
                                       w0r, a0r, wup, aup, gup)
        pad8 = lambda t: jnp.pad(t.reshape(Bs, Ts, D), ((0, 0), (0, TS8 - Ts), (0, 0))).reshape(Bs * TS8, D)
        ya8, s_fin = _rwkv_scan(l, pad8(r), pad8(k), pad8(v), pad8(lw), pad8(a), pad8(g), *scan_par,
                                state_wkv, Bs, TS8, TS8)
        ya = ya8.reshape(Bs, TS8, D)[:, :Ts].reshape(Ms, D)
        p_new = P[:, o3:o3 + D].reshape(Bs, Ts, D)
        ext = jnp.concatenate([state_pool[l], p_new], axis=1)
        extp = jnp.pad(ext, ((0, 0), (PG - POOL_BUF - Ts, 0), (0, 0))).reshape(Bs * PG, D)
        yb = _pool_ext(extp, l, D, pw16, psr, group_rows=PG, out_row0=PG - Ts, pos0=PAST_LEN)
        yb = yb.reshape(Bs, PG, D)[:, PG - Ts:].reshape(Ms, D)
        y1 = _merge_out(P, l, D, ya, yb, wout16)
        xs, z2 = _resnorm(xs, gain(n2r, l), res=(y1, mod_s, l, 2, 1), mod=(mod_s, l, 4, 3, 1), tm=tms)
        ys = peer(z2)
        outs["wkv_s"].append(s_fin)
        outs["shift_s"].append(shift_out(P, Bs, Ts))
        outs["pool_s"].append(ext[:, -POOL_BUF:])

    fin = (final_g.reshape(1, D), pl.BlockSpec((1, D), lambda i: (0, 0)))
    _, y_prompt = _resnorm(xp, fin, res=(yp, mod_p, L - 1, 5, bpg), z_dtype=F32, tm=tmn)
    _, y_sample = _resnorm(xs, fin, res=(ys, mod_s, L - 1, 5, 1), z_dtype=F32, tm=tms)
    st = lambda name: jnp.stack(outs[name], axis=0)
    return (y_prompt.reshape(Bp, Tp, D), y_sample.reshape(Bs, Ts, D),
            st("wkv_p"), st("shift_p"), st("pool_p"), st("wkv_s"), st("shift_s"), st("pool_s"))
```

```python
import functools

import jax
import jax.numpy as jnp
from jax import lax
from jax.experimental import pallas as pl
from jax.experimental.pallas import tpu as pltpu

F32 = jnp.float32
BF16 = jnp.bfloat16

HEAD_DIM = 64
LANES = 128
MXU_WIDTH = 256
GN_EPS = 64e-5
NORM_EPS = 1e-6
PAST_LEN = 16384
POOL_WINDOWS = (2, 4, 8, 16)
POOL_BUF = 15
PEER_TOPK = 16
N_KEYS = 128
SCAN_ROWS = 64
VMEM_LIMIT = 56 * 1024 * 1024


def _cp(*sem):
    return pltpu.CompilerParams(dimension_semantics=sem, vmem_limit_bytes=VMEM_LIMIT)


def _rup(n, m):
    return (n + m - 1) // m * m


def _dot(a, b):
    return jnp.dot(a, b, preferred_element_type=F32)


def _dot_nt(a, b):
    return lax.dot_general(a, b, (((1,), (1,)), ((), ())), preferred_element_type=F32)


def _lspec(l, block, index_map):
    return pl.BlockSpec((None,) + tuple(block), lambda *g: (l,) + tuple(index_map(*g)))


def _mod_kernel(c_ref, w_ref, b_ref, o_ref):
    c = c_ref[...]
    a = (c * jax.nn.sigmoid(c)).astype(BF16)
    o_ref[...] = _dot(a, w_ref[...].astype(BF16)) + b_ref[...]


def _modulation(c_all, w_mod, b_mod):
    L, D, N = w_mod.shape
    Mc = c_all.shape[0]
    tn = 512
    return pl.pallas_call(
        _mod_kernel,
        grid=(L, N // tn),
        in_specs=[
            pl.BlockSpec((Mc, D), lambda l, j: (0, 0)),
            pl.BlockSpec((None, D, tn), lambda l, j: (l, 0, j)),
            pl.BlockSpec((None, 1, tn), lambda l, j: (l, 0, j)),
        ],
        out_specs=pl.BlockSpec((None, Mc, tn), lambda l, j: (l, 0, j)),
        out_shape=jax.ShapeDtypeStruct((L, Mc, N), F32),
        compiler_params=_cp("parallel", "parallel"),
        name="modulation",
    )(c_all, w_mod, b_mod.reshape(L, 1, N))


def _resnorm_kernel(*refs, has_res, has_mod, y_t, z_t):
    it = iter(refs)
    x_ref = next(it)
    if has_res:
        y_ref = next(it)
        gt_ref = next(it)
    g_ref = next(it)
    if has_mod:
        sc_ref = next(it)
        sh_ref = next(it)
    if has_res:
        xo_ref = next(it)
    z_ref = next(it)
    x = x_ref[...]
    if has_res:
        y = y_ref[...].T if y_t else y_ref[...]
        x = x + gt_ref[...] * y
        xo_ref[...] = x
    ms = jnp.mean(x * x, axis=-1, keepdims=True)
    n = x * lax.rsqrt(ms + NORM_EPS) * g_ref[...]
    if has_mod:
        n = n * (1.0 + sc_ref[...]) + sh_ref[...]
    z_ref[...] = (n.T if z_t else n).astype(z_ref.dtype)


def _resnorm(x, g_spec, *, res=None, mod=None, z_dtype=BF16, tm=256, y_t=False, z_t=False):
    M, D = x.shape
    tm = min(tm, M)
    row = pl.BlockSpec((tm, D), lambda i: (i, 0))
    col = pl.BlockSpec((D, tm), lambda i: (0, i))

    def mspec(m4, l, piece, bpg):
        return pl.BlockSpec((None, None, m4.shape[2], D), lambda i: (l, i // bpg, 0, piece))

    args, specs = [x], [row]
    if res is not None:
        y, m4, l, piece, bpg = res
        args += [y, m4]
        specs += [col if y_t else row, mspec(m4, l, piece, bpg)]
    args.append(g_spec[0])
    specs.append(g_spec[1])
    if mod is not None:
        m4, l, scp, shp, bpg = mod
        args += [m4, m4]
        specs += [mspec(m4, l, scp, bpg), mspec(m4, l, shp, bpg)]
    out_shape, out_specs = [], []
    if res is not None:
        out_shape.append(jax.ShapeDtypeStruct((M, D), F32))
        out_specs.append(row)
    out_shape.append(jax.ShapeDtypeStruct((D, M) if z_t else (M, D), z_dtype))
    out_specs.append(col if z_t else row)
    outs = pl.pallas_call(
        functools.partial(_resnorm_kernel, has_res=res is not None, has_mod=mod is not None, y_t=y_t, z_t=z_t),
        grid=(M // tm,),
        in_specs=specs,
        out_specs=out_specs,
        out_shape=out_shape,
        compiler_params=_cp("parallel"),
        name="resnorm",
    )(*args)
    return outs if res is not None else (None, outs[0])


def _mm_kernel(a_ref, b_ref, o_ref):
    o_ref[...] = _dot(a_ref[...], b_ref[...]).astype(o_ref.dtype)


def _mm(a, b, *, la=None, lb=None, tm=1024, tn=512, out_dtype=F32):
    M, K = a.shape[-2:]
    N = b.shape[-1]
    tm, tn = min(tm, M), min(tn, N)
    a_spec = (pl.BlockSpec((tm, K), lambda i, j: (i, 0)) if la is None
              else _lspec(la, (tm, K), lambda i, j: (i, 0)))
    b_spec = (pl.BlockSpec((K, tn), lambda i, j: (0, j)) if lb is None
              else _lspec(lb, (K, tn), lambda i, j: (0, j)))
    return pl.pallas_call(
        _mm_kernel,
        grid=(M // tm, N // tn),
        in_specs=[a_spec, b_spec],
        out_specs=pl.BlockSpec((tm, tn), lambda i, j: (i, j)),
        out_shape=jax.ShapeDtypeStruct((M, N), out_dtype),
        compiler_params=_cp("parallel", "parallel"),
        name="mm",
    )(a, b)


def _softplus(u):
    return jnp.maximum(u, 0.0) + jnp.log1p(jnp.exp(-jnp.abs(u)))


def _prep_kernel(*refs, seq_len, per_row_state, lw, la):
    (r_ref, k_ref, v_ref, l_ref, hr_ref, hk_ref, hv_ref, hl_ref) = refs[:8]
    n = 8
    if per_row_state:
        sr_ref, sk_ref, sv_ref, sl_ref = refs[n:n + 4]
        n += 4
    (mur_ref, muk_ref, muv_ref, mul_ref, w0_ref, a0_ref, wup_ref, aup_ref, gup_ref) = refs[n:n + 9]
    ro_ref, ko_ref, vo_ref, lwo_ref, ao_ref, go_ref = refs[n + 9:]
    tm = r_ref.shape[0]
    base = pl.program_id(0) * tm
    row = lax.broadcasted_iota(jnp.int32, (tm, 1), 0)
    pos = (base + row) % seq_len
    first = row == 0
    start = pos == 0

    def mixed(x_ref, h_ref, s_ref, mu_ref):
        x = x_ref[...]
        prev = jnp.where(first, h_ref[7:8, :], pltpu.roll(x, 1, 0))
        if per_row_state:
            prev = jnp.where(start, s_ref[...], prev)
        else:
            prev = jnp.where(start, 0.0, prev)
        return x + mu_ref[...] * (prev - x)

    sr = sk = sv = sl = None
    if per_row_state:
        sr, sk, sv, sl = sr_ref, sk_ref, sv_ref, sl_ref
    r = mixed(r_ref, hr_ref, sr, mur_ref)
    k = mixed(k_ref, hk_ref, sk, muk_ref)
    v = mixed(v_ref, hv_ref, sv, muv_ref)
    lo = mixed(l_ref, hl_ref, sl, mul_ref)
    wd = lo[:, :lw]
    ad = lo[:, lw:lw + la]
    gd = lo[:, lw + la:]
    wlin = w0_ref[...] + _dot(jnp.tanh(wd).astype(BF16), wup_ref[...])
    w_log = -_softplus(-wlin) - 0.5
    ro_ref[...] = r
    ko_ref[...] = k
    vo_ref[...] = v
    lwo_ref[...] = -jnp.exp(w_log)
    ao_ref[...] = jax.nn.sigmoid(a0_ref[...] + _dot(ad.astype(BF16), aup_ref[...]))
    go_ref[...] = _dot(jax.nn.sigmoid(gd).astype(BF16), gup_ref[...])


def _rwkv_prep(P, l, D, LB, lw, la, seq_len, state_rows, mu_rkv, mu_l, w0, a0, wup, aup, gup, tm=128):
    M = P.shape[0]
    tm = min(tm, M)
    lcol = 6 * D // LB
    per_row_state = state_rows is not None

    def cur(w, c):
        return pl.BlockSpec((tm, w), lambda i: (i, c))

    def halo(w, c):
        return pl.BlockSpec((8, w), lambda i: (jnp.maximum(i * (tm // 8) - 1, 0), c))

    args = [P] * 8
    specs = [cur(D, 0), cur(D, 1), cur(D, 2), cur(LB, lcol),
             halo(D, 0), halo(D, 1), halo(D, 2), halo(LB, lcol)]
    if per_row_state:
        s_rkv, s_l = state_rows
        args += [s_rkv, s_rkv, s_rkv, s_l]
        specs += [_lspec(l, (tm, D), lambda i, c=c: (i, c)) for c in range(3)]
        specs += [_lspec(l, (tm, LB), lambda i: (i, 0))]

    def const(arr, c=0, w=None):
        w = arr.shape[-1] if w is None else w
        return _lspec(l, (arr.shape[-2], w), lambda i: (0, c))

    args += [mu_rkv, mu_rkv, mu_rkv, mu_l, w0, a0, wup, aup, gup]
    specs += [const(mu_rkv, 0, D), const(mu_rkv, 1, D), const(mu_rkv, 2, D), const(mu_l),
              const(w0), const(a0), const(wup), const(aup), const(gup)]
    out = jax.ShapeDtypeStruct((M, D), F32)
    return pl.pallas_call(
        functools.partial(_prep_kernel, seq_len=seq_len, per_row_state=per_row_state, lw=lw, la=la),
        grid=(M // tm,),
        in_specs=specs,
        out_specs=[pl.BlockSpec((tm, D), lambda i: (i, 0))] * 6,
        out_shape=[out] * 6,
        compiler_params=_cp("parallel"),
        name="rwkv_prep",
    )(*args)


def _split3(x):
    h = x.astype(BF16)
    r1 = x - h.astype(F32)
    m = r1.astype(BF16)
    lo = (r1 - m.astype(F32)).astype(BF16)
    return h, m, lo


def _scan_kernel(*refs, G, Tc, zero_init):
    if zero_init:
        (r_ref, k_ref, v_ref, lw_ref, a_ref, g_ref, kk_ref, ka_ref, rk_ref, lnw_ref, lnb_ref,
         y_ref, so_ref, st_ref) = refs
    else:
        (r_ref, k_ref, v_ref, lw_ref, a_ref, g_ref, kk_ref, ka_ref, rk_ref, lnw_ref, lnb_ref,
         s0_ref, y_ref, so_ref, st_ref) = refs
    C = SCAN_ROWS
    nseq = C // Tc
    Q = range(G)

    @pl.when(pl.program_id(2) == 0)
    def _():
        if zero_init:
            st_ref[...] = jnp.zeros_like(st_ref)
        else:
            z = jnp.zeros((HEAD_DIM, HEAD_DIM), F32)
            for s in range(nseq):
                for q in Q:
                    top = jnp.concatenate([s0_ref[s, 2 * q], z], axis=1)
                    bot = jnp.concatenate([z, s0_ref[s, 2 * q + 1]], axis=1)
                    st_ref[s, q] = jnp.concatenate([top, bot], axis=0)

    lane = lax.broadcasted_iota(jnp.int32, (1, LANES), 1)
    m_a = lane < HEAD_DIM
    fa = m_a.astype(F32)
    fb = 1.0 - fa
    ri = lax.broadcasted_iota(jnp.int32, (C, LANES), 0)
    cj = lax.broadcasted_iota(jnp.int32, (C, LANES), 1)
    sj = jnp.where(cj >= C, cj - C, cj)
    far = 1 << 20
    sj = jnp.where((ri // Tc) == (sj // Tc), sj, far)
    strict = sj < ri
    incl = sj <= ri
    col_a = (cj < C).astype(F32)
    col_b = 1.0 - col_a
    ti = lax.broadcasted_iota(jnp.int32, (C, 3 * C), 0)
    tj = lax.broadcasted_iota(jnp.int32, (C, 3 * C), 1) % C
    tri3 = jnp.where(jnp.where((ti // Tc) == (tj // Tc), tj, far) <= ti, 1.0, 0.0).astype(BF16)
    bi = lax.broadcasted_iota(jnp.int32, (LANES, LANES), 0)
    bj = lax.broadcasted_iota(jnp.int32, (LANES, LANES), 1)
    bd = (bi < HEAD_DIM) == (bj < HEAD_DIM)
    eye = (bi == bj).astype(F32)
    eye16 = eye.astype(BF16)
    levels = Tc.bit_length() - 1

    def head_sum(x):
        sa = jnp.sum(x * fa, axis=1, keepdims=True)
        sb = jnp.sum(x * fb, axis=1, keepdims=True)
        return jnp.where(m_a, sa, sb)

    def stack(x):
        return jnp.concatenate([x * fa, x * fb], axis=0)

    def seq_rows(x, s):
        return x[s * Tc:(s + 1) * Tc]

    sls = [slice(q * LANES, (q + 1) * LANES) for q in Q]
    r = [r_ref[:, sl] for sl in sls]
    k = [k_ref[:, sl] for sl in sls]
    v = [v_ref[:, sl] for sl in sls]
    lw = [lw_ref[:, sl] for sl in sls]
    a = [a_ref[:, sl] for sl in sls]
    states = [[st_ref[s, q] for s in range(nseq)] for q in Q]
    cum = [_dot(tri3, jnp.concatenate(_split3(lw[q]), axis=0)) for q in Q]
    kk = [k[q] * kk_ref[:, sls[q]] for q in Q]
    kk = [kk[q] / jnp.maximum(jnp.sqrt(head_sum(kk[q] * kk[q])), 1e-12) for q in Q]
    km = [k[q] * (1.0 + (a[q] - 1.0) * ka_ref[:, sls[q]]) for q in Q]
    wcum = [jnp.exp(cum[q]) for q in Q]
    winv = [jnp.exp(-cum[q]) for q in Q]
    rt = [r[q] * wcum[q] for q in Q]
    at = [-kk[q] * jnp.exp(cum[q] - lw[q]) for q in Q]
    bt = [kk[q] * a[q] * winv[q] for q in Q]
    kt = [km[q] * winv[q] for q in Q]
    x16 = [jnp.concatenate([at[q], rt[q]], axis=0).astype(BF16) for q in Q]
    vst = [stack(v[q]).astype(BF16) for q in Q]
    amat = [_dot_nt(x16[q], jnp.concatenate([stack(bt[q]), stack(kt[q])], axis=0).astype(BF16)) for q in Q]
    if nseq == 1:
        xs = [_dot_nt(x16[q], states[q][0].astype(BF16)) for q in Q]
        xs_a = [xs[q][:C] for q in Q]
        xs_r = [xs[q][C:] for q in Q]
    else:
        parts = [[_dot_nt(jnp.concatenate([seq_rows(at[q], s), seq_rows(rt[q], s)], axis=0).astype(BF16),
                          states[q][s].astype(BF16)) for s in range(nseq)] for q in Q]
        xs_a = [jnp.concatenate([p_[:Tc] for p_ in parts[q]], axis=0) for q in Q]
        xs_r = [jnp.concatenate([p_[Tc:] for p_ in parts[q]], axis=0) for q in Q]
    a_ab = [jnp.where(strict, amat[q][:C, :LANES], 0.0) for q in Q]
    a_ak = [jnp.where(strict, amat[q][:C, LANES:], 0.0) for q in Q]
    a_rb = [jnp.where(incl, amat[q][C:, :LANES], 0.0) for q in Q]
    a_rk = [jnp.where(incl, amat[q][C:, LANES:], 0.0) for q in Q]
    rhs = [xs_a[q] + _dot(a_ak[q].astype(BF16), vst[q]) for q in Q]
    nbd = [jnp.concatenate([a_ab[q] * col_a, a_ab[q] * col_b], axis=0) for q in Q]
    tinv = [eye + nbd[q] for q in Q]
    if levels > 1:
        n16 = [nbd[q].astype(BF16) for q in Q]
        pw = [_dot(n16[q], n16[q]) for q in Q]
        for lev in range(1, levels):
            p16 = [pw[q].astype(BF16) for q in Q]
            if lev + 1 < levels:
                pt = [_dot(p16[q], jnp.concatenate([p16[q], tinv[q].astype(BF16)], axis=1)) for q in Q]
                pw = [pt[q][:, :LANES] for q in Q]
                tinv = [tinv[q] + pt[q][:, LANES:] for q in Q]
            else:
                tinv = [tinv[q] + _dot(p16[q], tinv[q].astype(BF16)) for q in Q]
    ust = [_dot(tinv[q].astype(BF16), stack(rhs[q]).astype(BF16)) for q in Q]
    u = [ust[q][:C] + ust[q][C:] for q in Q]
    y = [xs_r[q] + _dot(jnp.concatenate([a_rb[q], a_rk[q]], axis=1).astype(BF16),
                        jnp.concatenate([ust[q].astype(BF16), vst[q]], axis=0)) for q in Q]
    uvt = [[_dot_nt(eye16, jnp.concatenate([seq_rows(u[q], s), seq_rows(v[q], s)], axis=0).astype(BF16))
            for s in range(nseq)] for q in Q]
    upd = [[_dot(uvt[q][s].astype(BF16),
                 jnp.concatenate([seq_rows(bt[q], s), seq_rows(kt[q], s)], axis=0).astype(BF16))
            for s in range(nseq)] for q in Q]
    for q in Q:
        for s in range(nseq):
            st_ref[s, q] = ((states[q][s] + jnp.where(bd, upd[q][s], 0.0))
                            * wcum[q][(s + 1) * Tc - 1:(s + 1) * Tc, :])
    for q in Q:
        sl = sls[q]
        mean = head_sum(y[q]) * (1.0 / HEAD_DIM)
        d = y[q] - mean
        var = head_sum(d * d) * (1.0 / HEAD_DIM)
        yn = d * lax.rsqrt(var + GN_EPS) * lnw_ref[:, sl] + lnb_ref[:, sl]
        bonus = head_sum(r[q] * km[q] * rk_ref[:, sl]) * v[q]
        y_ref[:, sl] = (yn + bonus) * g_ref[:, sl]

    @pl.when(pl.program_id(2) == pl.num_programs(2) - 1)
    def _():
        for s in range(nseq):
            for q in Q:
                full = st_ref[s, q]
                so_ref[s, 2 * q] = full[:HEAD_DIM, :HEAD_DIM]
                so_ref[s, 2 * q + 1] = full[HEAD_DIM:, HEAD_DIM:]


def _rwkv_scan(l, r, k, v, lw, a, g, k_k, k_a, r_k, ln_w, ln_b, s0, nstates, Tseq, Tc, G=16):
    M, D = r.shape
    C = SCAN_ROWS
    nseq = C // Tc
    nc = Tseq // Tc
    assert 2 * C == LANES and (nc == 1 or nseq == 1)
    G = min(G, D // LANES)
    H = D // HEAD_DIM
    tok = pl.BlockSpec((C, G * LANES), lambda b, p, c: (b * nc + c, p))
    par = _lspec(l, (1, G * LANES), lambda b, p, c: (0, p))
    st_out = pl.BlockSpec((nseq, 2 * G, HEAD_DIM, HEAD_DIM), lambda b, p, c: (b, p, 0, 0))
    args = [r, k, v, lw, a, g, k_k, k_a, r_k, ln_w, ln_b]
    specs = [tok] * 6 + [par] * 5
    if s0 is not None:
        args.append(s0)
        specs.append(_lspec(l, (nseq, 2 * G, HEAD_DIM, HEAD_DIM), lambda b, p, c: (b, p, 0, 0)))
    return pl.pallas_call(
        functools.partial(_scan_kernel, G=G, Tc=Tc, zero_init=s0 is None),
        grid=(M // (C * nc), D // (G * LANES), nc),
        in_specs=specs,
        out_specs=[tok, st_out],
        out_shape=[jax.ShapeDtypeStruct((M, D), F32),
                   jax.ShapeDtypeStruct((nstates, H, HEAD_DIM, HEAD_DIM), F32)],
        scratch_shapes=[pltpu.VMEM((nseq, G, LANES, LANES), F32)],
        compiler_params=_cp("parallel", "parallel", "arbitrary"),
        name="rwkv_scan",
    )(*args)


def _pool_kernel(*refs, has_halo, seq_len, group_rows, out_row0, pos0, gd):
    if has_halo:
        p_ref, h_ref, pw_ref, ps_ref, o_ref = refs
    else:
        p_ref, pw_ref, ps_ref, o_ref = refs
    tm = p_ref.shape[0]
    if has_halo:
        base = (pl.program_id(0) * tm) % seq_len
        halo = jnp.where(base == 0, 0.0, h_ref[...])
        ext = jnp.concatenate([halo, p_ref[...]], axis=0)
        off = h_ref.shape[0]
        row = lax.broadcasted_iota(jnp.int32, (tm, 1), 0)
        pos = base + row + pos0
    else:
        ext = p_ref[...]
        off = 0
        row = lax.broadcasted_iota(jnp.int32, (tm, 1), 0)
        pos = pos0 + (row % group_rows) - out_row0
    for gi, w in enumerate(POOL_WINDOWS):
        x = ext[:, gi * gd:(gi + 1) * gd]
        s = x
        span = 1
        while span < w:
            s = s + pltpu.roll(s, span, 0)
            span *= 2
        cnt = jnp.minimum(w, pos + 1).astype(F32)
        mixed = s[off:] / cnt - x[off:]
        y = _dot(mixed.astype(BF16), pw_ref[gi])
        o_ref[:, gi * gd:(gi + 1) * gd] = y * ps_ref[:, gi * gd:(gi + 1) * gd]


def _pool(P, l, pcol, D, pw16, ps, *, seq_len, tm=256):
    M = P.shape[0]
    tm = min(tm, seq_len)
    G = len(POOL_WINDOWS)
    gd = D // G
    return pl.pallas_call(
        functools.partial(_pool_kernel, has_halo=True, seq_len=seq_len, group_rows=1, out_row0=0,
                          pos0=0, gd=gd),
        grid=(M // tm,),
        in_specs=[pl.BlockSpec((tm, D), lambda i: (i, pcol)),
                  pl.BlockSpec((16, D), lambda i: (jnp.maximum(i * (tm // 16) - 1, 0), pcol)),
                  _lspec(l, (G, gd, gd), lambda i: (0, 0, 0)),
                  _lspec(l, (1, D), lambda i: (0, 0))],
        out_specs=pl.BlockSpec((tm, D), lambda i: (i, 0)),
        out_shape=jax.ShapeDtypeStruct((M, D), F32),
        compiler_params=_cp("parallel"),
        name="pool",
    )(P, P, pw16, ps)


def _pool_ext(ext, l, D, pw16, ps, *, group_rows, out_row0, pos0, tm=384):
    M = ext.shape[0]
    tm = min(tm, M)
    G = len(POOL_WINDOWS)
    gd = D // G
    return pl.pallas_call(
        functools.partial(_pool_kernel, has_halo=False, seq_len=1, group_rows=group_rows,
                          out_row0=out_row0, pos0=pos0, gd=gd),
        grid=(M // tm,),
        in_specs=[pl.BlockSpec((tm, D), lambda i: (i, 0)),
                  _lspec(l, (G, gd, gd), lambda i: (0, 0, 0)),
                  _lspec(l, (1, D), lambda i: (0, 0))],
        out_specs=pl.BlockSpec((tm, D), lambda i: (i, 0)),
        out_shape=jax.ShapeDtypeStruct((M, D), F32),
        compiler_params=_cp("parallel"),
        name="pool_ext",
    )(ext, pw16, ps)


def _merge_out_kernel(ga_ref, gb_ref, ya_ref, yb_ref, w_ref, o_ref, m_scr):
    @pl.when(pl.program_id(1) == 0)
    def _():
        m = jax.nn.sigmoid(ga_ref[...]) * ya_ref[...] + jax.nn.sigmoid(gb_ref[...]) * yb_ref[...]
        m_scr[...] = m.astype(BF16)

    o_ref[...] = _dot(m_scr[...], w_ref[...])


def _merge_out(P, l, D, ya, yb, wout16, tm=512, tn=512):
    M = P.shape[0]
    tm, tn = min(tm, M), min(tn, D)
    tok = pl.BlockSpec((tm, D), lambda i, j: (i, 0))
    return pl.pallas_call(
        _merge_out_kernel,
        grid=(M // tm, D // tn),
        in_specs=[pl.BlockSpec((tm, D), lambda i, j: (i, 4)),
                  pl.BlockSpec((tm, D), lambda i, j: (i, 5)),
                  tok, tok,
                  _lspec(l, (D, tn), lambda i, j: (0, j))],
        out_specs=pl.BlockSpec((tm, tn), lambda i, j: (i, j)),
        out_shape=jax.ShapeDtypeStruct((M, D), F32),
        scratch_shapes=[pltpu.VMEM((tm, D), BF16)],
        compiler_params=_cp("parallel", "arbitrary"),
        name="merge_out",
    )(P, P, ya, yb, wout16)


PEER_RANKS = PEER_TOPK + 1
PEER_RANK_ROWS = _rup(PEER_RANKS, 8)
PEER_PAIR_COUNTS = tuple(min(PEER_RANKS, PEER_RANKS // (i + 1)) for i in range(PEER_RANKS))
PEER_CAND_ROWS = _rup(sum(PEER_PAIR_COUNTS), 8)


def _peer_select_kernel(qt_ref, keys_ref, ethr_ref, e1_ref, e2_ref,
                        s1_scr, s2_ref, t1_scr, t2_scr, cand_scr, top_scr, *, heads):
    half = N_KEYS
    tl = qt_ref.shape[1]
    neg = -jnp.inf
    for h in range(heads):
        q1 = qt_ref[2 * h * half:(2 * h + 1) * half, :].astype(BF16)
        q2 = qt_ref[(2 * h + 1) * half:(2 * h + 2) * half, :].astype(BF16)
        s1_scr[h] = _dot(keys_ref[0], q1)
        s2_ref[h] = _dot(keys_ref[1], q2)

    def next_max(x, prev):
        return jnp.max(jnp.where(x < prev, x, neg), axis=0, keepdims=True)

    def rank_body(i, prev):
        new = []
        for h in range(heads):
            m1 = next_max(s1_scr[h], prev[2 * h])
            m2 = next_max(s2_ref[h], prev[2 * h + 1])
            t1_scr[h, pl.ds(i, 1), :] = m1
            t2_scr[h, pl.ds(i, 1), :] = m2
            new += [m1, m2]
        return tuple(new)

    inf_row = jnp.full((1, tl), jnp.inf, F32)
    lax.fori_loop(0, PEER_RANKS, rank_body, (inf_row,) * (2 * heads))

    for h in range(heads):
        cand_scr[h, PEER_CAND_ROWS - 8:, :] = jnp.full((8, tl), neg, F32)
        r0 = 0
        for i, n in enumerate(PEER_PAIR_COUNTS):
            cand_scr[h, r0:r0 + n, :] = t1_scr[h, i:i + 1, :] + t2_scr[h, 0:n, :]
            r0 += n

    def cand_body(i, prev):
        new = []
        for h in range(heads):
            m = next_max(cand_scr[h], prev[h])
            top_scr[h, pl.ds(i, 1), :] = m
            new.append(m)
        return tuple(new)

    lax.fori_loop(0, PEER_RANKS, cand_body, (inf_row,) * heads)

    for h in range(heads):
        top = top_scr[h]
        t16 = top[PEER_TOPK - 1:PEER_TOPK]
        t17 = top[PEER_TOPK:PEER_TOPK + 1]
        mid = jnp.where(t17 > neg, 0.5 * (t16 + t17), t16)
        z = jnp.sum(jnp.exp(top[:PEER_TOPK] - top[0:1]), axis=0, keepdims=True)
        s1 = s1_scr[h]
        m2 = t2_scr[h, 0:1, :]
        ethr_ref[h] = jnp.exp((mid - m2) - s1)
        e1_ref[h] = jnp.exp(s1 - t1_scr[h, 0:1, :]) / z
        e2_ref[h] = jnp.exp(s2_ref[h] - m2)


def _peer_select(qt, keys16, l, heads, tl=128):
    QD, M = qt.shape
    tl = min(tl, M)
    big = jax.ShapeDtypeStruct((heads, N_KEYS, M), F32)
    bspec = pl.BlockSpec((heads, N_KEYS, tl), lambda i: (0, 0, i))
    ranks = pltpu.VMEM((heads, PEER_RANK_ROWS, tl), F32)
    return pl.pallas_call(
        functools.partial(_peer_select_kernel, heads=heads),
        grid=(M // tl,),
        in_specs=[pl.BlockSpec((QD, tl), lambda i: (0, i)),
                  _lspec(l, keys16.shape[1:], lambda i: (0, 0, 0))],
        out_specs=[bspec] * 3,
        out_shape=[big] * 3,
        scratch_shapes=[pltpu.VMEM((heads, N_KEYS, tl), F32), pltpu.VMEM((heads, N_KEYS, tl), F32), ranks, ranks,
                        pltpu.VMEM((heads, PEER_CAND_ROWS, tl), F32), ranks],
        compiler_params=_cp("parallel"),
        name="peer_select",
    )(qt, keys16)


PEER_LANE_TILE = 128
PEER_STEP_KEYS = 4
PEER_UNIT_KEYS = 1
PEER_UNIT_GROUPS = 16


def _gelu_tanh(x):
    return 0.5 * x * (1.0 + jnp.tanh(0.7978845608028654 * (x + 0.044715 * (x * x * x))))


def _peer_dense_kernel(zt_ref, u_ref, vt_ref, thr_ref, e1_ref, e2_ref, o_ref, ht0, ht1, wt0, wt1, *, heads, nblk):
    s = pl.program_id(1)
    tm = zt_ref.shape[1]
    lt = min(PEER_LANE_TILE, tm)
    ka = PEER_STEP_KEYS
    uk = PEER_UNIT_KEYS
    groups = N_KEYS // 8
    mxu_n = min(MXU_WIDTH, tm)
    n_lane = tm // mxu_n
    d_out = o_ref.shape[0]
    ug = PEER_UNIT_GROUPS
    units = [(t0, ap, g0) for t0 in range(0, tm, lt) for ap in range(0, ka, uk) for g0 in range(0, groups, ug)]
    nu = len(units)

    @pl.when(s == 0)
    def _():
        o_ref[...] = jnp.zeros_like(o_ref)
        for buf in (ht0, ht1, wt0, wt1):
            buf[...] = jnp.zeros_like(buf)

    valid = jnp.logical_and(s >= 1, s <= nblk)

    def order_token(res):
        bits = pltpu.bitcast(res[:8, :lt], jnp.int32)
        bits = lax.shift_right_logical(lax.shift_right_logical(bits, 16), 16)
        return bits.astype(F32)

    def body(ht_new, ht_old, wt_new, wt_old, k0):
        n_sc = 2 * n_lane
        sc_rows = ka * N_KEYS // 2
        n_cb = 4 * n_lane
        ob = d_out // (n_cb // n_lane)

        def scores(piece):
            rs = slice((piece // n_lane) * sc_rows, (piece // n_lane + 1) * sc_rows)
            ts = slice((piece % n_lane) * mxu_n, (piece % n_lane + 1) * mxu_n)
            res = _dot(u_ref[rs, :], zt_ref[:, ts])
            ht_new[rs, ts] = res
            return order_token(res)

        def combine(piece):
            os_ = slice((piece // n_lane) * ob, (piece // n_lane + 1) * ob)
            ts = slice((piece % n_lane) * mxu_n, (piece % n_lane + 1) * mxu_n)
            res = _dot(vt_ref[os_, :], wt_new[:, ts])
            o_ref[os_, ts] += res
            return order_token(res)

        def weights(unit, start):
            t0, ap, g0 = units[unit]
            ts = slice(t0, t0 + lt)
            accs = [[start for _ in range(ug)] for _ in range(uk)]
            for h in range(heads):
                thr_b = [jnp.broadcast_to(thr_ref[h, k0 + ap + d:k0 + ap + d + 1, ts], (8, lt)) for d in range(uk)]
                e1_b = [jnp.broadcast_to(e1_ref[h, k0 + ap + d:k0 + ap + d + 1, ts], (8, lt)) for d in range(uk)]
                for g in range(ug):
                    e2t = e2_ref[h, (g0 + g) * 8:(g0 + g + 1) * 8, ts]
                    for d in range(uk):
                        accs[d][g] = accs[d][g] + jnp.where(e2t >= thr_b[d], e1_b[d] * e2t, 0.0)
            for d in range(uk):
                for g in range(0, ug, 2):
                    rs = slice((ap + d) * N_KEYS + (g0 + g) * 8, (ap + d) * N_KEYS + (g0 + g + 2) * 8)
                    w = jnp.concatenate([accs[d][g], accs[d][g + 1]], axis=0) * _gelu_tanh(ht_old[rs, ts])
                    wt_old[rs, ts] = jnp.where(valid, w, 0.0).astype(BF16)

        zero = jnp.zeros((8, lt), F32)
        pending, ready = [], []
        sc_every, cb_every = max(nu // n_sc, 1), max(nu // n_cb, 1)
        for unit in range(nu):
            if unit % cb_every == 0 and unit // cb_every < n_cb:
                pending.append(combine(unit // cb_every))
            if unit % sc_every == 0 and unit // sc_every < n_sc:
                pending.append(scores(unit // sc_every))
            start = zero
            for tok in ready:
                start = start + tok
            ready, pending = pending, []
            weights(unit, start)

    @pl.when(s % 2 == 0)
    def _():
        body(ht0, ht1, wt0, wt1, (8 - ka) % 8)

    @pl.when(s % 2 == 1)
    def _():
        body(ht1, ht0, wt1, wt0, 0)


def _peer_dense(zt, u16, vt16, l, ethr, e1, e2, tm=1024):
    D, M = zt.shape
    E = u16.shape[1]
    heads = e2.shape[0]
    tm = min(tm, M)
    ka = PEER_STEP_KEYS
    ec = ka * N_KEYS
    nblk = E // ec
    blk = lambda s_, back: jnp.clip(s_ - back, 0, nblk - 1)
    assert ka in (4, 8)
    rows_prev = pl.BlockSpec((heads, 8, tm), lambda i, s_: (0, blk(s_, 1) * ka // 8, i))
    buf_f32 = pltpu.VMEM((ec, tm), F32)
    buf_b16 = pltpu.VMEM((ec, tm), BF16)
    return pl.pallas_call(
        functools.partial(_peer_dense_kernel, heads=heads, nblk=nblk),
        grid=(M // tm, nblk + 2),
        in_specs=[pl.BlockSpec((D, tm), lambda i, s_: (0, i)),
                  _lspec(l, (ec, D), lambda i, s_: (blk(s_, 0), 0)),
                  _lspec(l, (D, ec), lambda i, s_: (0, blk(s_, 2))),
                  rows_prev, rows_prev,
                  pl.BlockSpec((heads, N_KEYS, tm), lambda i, s_: (0, 0, i))],
        out_specs=pl.BlockSpec((D, tm), lambda i, s_: (0, i)),
        out_shape=jax.ShapeDtypeStruct((D, M), F32),
        scratch_shapes=[buf_f32, buf_f32, buf_b16, buf_b16],
        compiler_params=_cp("parallel", "arbitrary"),
        name="peer_dense",
    )(zt, u16, vt16, ethr, e1, e2)


def _pad_last(w, n):
    return jnp.pad(w, [(0, 0)] * (w.ndim - 1) + [(0, n - w.shape[-1])])


def _pad_rows(w, n):
    return jnp.pad(w, [(0, 0)] * (w.ndim - 2) + [(0, n - w.shape[-2]), (0, 0)])


def kernel(x_prompt, x_sample, c_prompt, c_sample, state_wkv, state_shift, state_pool, norm1_g, norm2_g, final_g, w_mod, b_mod, w_in, mu_shift, w0, w_up, a0, a_up, g_up, k_k, k_a, r_k, ln_w, ln_b, pool_w, pool_scale, w_out, w_q, sub_keys, peer_u, peer_v):
    Bp, Tp, D = x_prompt.shape
    Bs, Ts, _ = x_sample.shape
    L = w_in.shape[0]
    H = D // HEAD_DIM
    dw, da, dg = w_up.shape[1], a_up.shape[1], g_up.shape[1]
    LW, LA, LG = _rup(dw, LANES), _rup(da, LANES), _rup(dg, LANES)
    LB = LW + LA + LG
    o3 = 3 * D
    o4, o5, o6 = o3 + dw, o3 + dw + da, o3 + dw + da + dg
    peer_heads = w_q.shape[2] // (2 * N_KEYS)
    Mp, Ms = Bp * Tp, Bs * Ts
    TS8 = 8
    PG = 24
    assert Tp % SCAN_ROWS == 0 and Ts <= TS8 and POOL_BUF + Ts <= PG

    Mc = _rup(Bp + Bs, 8)
    c_all = jnp.pad(jnp.concatenate([c_prompt, c_sample], axis=0), ((0, Mc - Bp - Bs), (0, 0)))
    mod_all = _modulation(c_all, w_mod, b_mod)
    tmn, tms = min(256, Tp), min(256, Ms)
    bpg = Tp // tmn
    mod_p = mod_all[:, :Bp].reshape(L, Bp, 1, 6 * D)
    mod_s = jnp.repeat(mod_all[:, Bp:Bp + Bs], Ts, axis=1).reshape(L, Ms // tms, tms, 6 * D)

    def repack_cols(t):
        return t[..., :o3], jnp.concatenate(
            [_pad_last(t[..., o3:o4], LW), _pad_last(t[..., o4:o5], LA), _pad_last(t[..., o5:o6], LG)], axis=-1)

    w_rkv, w_l = repack_cols(w_in)
    winp = jnp.concatenate([w_rkv, w_in[..., o6:], w_l], axis=-1).astype(BF16)
    mu_rkv, mu_l = repack_cols(mu_shift[:, None, :])
    wup = _pad_rows(w_up, LW).astype(BF16)
    aup = _pad_rows(a_up, LA).astype(BF16)
    gup = _pad_rows(g_up, LG).astype(BF16)
    rows = lambda t: t.reshape(L, 1, D)
    w0r, a0r, kkr, kar, rkr, lnwr, lnbr, psr = (rows(t) for t in (w0, a0, k_k, k_a, r_k, ln_w, ln_b, pool_scale))
    n1r, n2r = rows(norm1_g), rows(norm2_g)
    pw16 = pool_w.astype(BF16)
    wout16 = w_out.astype(BF16)
    wqt16 = jnp.swapaxes(w_q, 1, 2).astype(BF16)
    keys16 = sub_keys.astype(BF16)
    u16 = peer_u.astype(BF16)
    vt16 = jnp.swapaxes(peer_v, 1, 2).astype(BF16)
    gain = lambda arr, l: (arr, _lspec(l, (1, D), lambda i: (0, 0)))
    st_rkv, st_l = repack_cols(state_shift)
    expand = lambda t: jnp.pad(t[:, :, None, :], ((0, 0), (0, 0), (0, Ts - 1), (0, 0))).reshape(L, Ms, -1)
    st_rkv, st_l = expand(st_rkv), expand(st_l)

    def shift_out(P, nseq, T):
        last = P.reshape(nseq, T, -1)[:, -1]
        return jnp.concatenate(
            [last[:, :o3], last[:, 6 * D:6 * D + dw], last[:, 6 * D + LW:6 * D + LW + da],
             last[:, 6 * D + LW + LA:6 * D + LW + LA + dg]], axis=-1)

    xp = x_prompt.reshape(Mp, D)
    xs = x_sample.reshape(Ms, D)
    yp = ys = None
    outs = {k_: [] for k_ in ("wkv_p", "shift_p", "pool_p", "wkv_s", "shift_s", "pool_s")}

    for l in range(L):
        def half_layer_1(x, y_prev, mod4, bpg_, tm_):
            res = None if y_prev is None else (y_prev, mod4, l - 1, 5, bpg_)
            x_new, z = _resnorm(x, gain(n1r, l), res=res, mod=(mod4, l, 1, 0, bpg_), tm=tm_, y_t=True)
            return (x if res is None else x_new), _mm(z, winp, lb=l)

        def peer(zt):
            qt = _mm(wqt16, zt, la=l, tm=512, tn=512)
            ethr, e1, e2 = _peer_select(qt, keys16, l, peer_heads)
            return _peer_dense(zt, u16, vt16, l, ethr, e1, e2)

        scan_par = (kkr, kar, rkr, lnwr, lnbr)

        xp, P = half_layer_1(xp, yp, mod_p, bpg, tmn)
        r, k, v, lw, a, g = _rwkv_prep(P, l, D, LB, LW, LA, Tp, None, mu_rkv, mu_l, w0r, a0r, wup, aup, gup)
        ya, s_fin = _rwkv_scan(l, r, k, v, lw, a, g, *scan_par, None, Bp, Tp, SCAN_ROWS)
        yb = _pool(P, l, 3, D, pw16, psr, seq_len=Tp)
        y1 = _merge_out(P, l, D, ya, yb, wout16)
        xp, z2 = _resnorm(xp, gain(n2r, l), res=(y1, mod_p, l, 2, bpg), mod=(mod_p, l, 4, 3, bpg), tm=tmn,
                          z_t=True)
        yp = peer(z2)
        outs["wkv_p"].append(s_fin)
        outs["shift_p"].append(shift_out(P, Bp, Tp))
        outs["pool_p"].append(P.reshape(Bp, Tp, -1)[:, Tp - POOL_BUF:, o3:o3 + D])

        xs, P = half_layer_1(xs, ys, mod_s, 1, tms)
        r, k, v, lw, a, g = _rwkv_prep(P, l, D, LB, LW, LA, Ts, (st_rkv, st_l), mu_rkv, mu_l,
                                       w0r, a0r, wup, aup, gup)
        pad8 = lambda t: jnp.pad(t.reshape(Bs, Ts, D), ((0, 0), (0, TS8 - Ts), (0, 0))).reshape(Bs * TS8, D)
        ya8, s_fin = _rwkv_scan(l, pad8(r), pad8(k), pad8(v), pad8(lw), pad8(a), pad8(g), *scan_par,
                                state_wkv, Bs, TS8, TS8, G=8)
        ya = ya8.reshape(Bs, TS8, D)[:, :Ts].reshape(Ms, D)
        p_new = P[:, o3:o3 + D].reshape(Bs, Ts, D)
        ext = jnp.concatenate([state_pool[l], p_new], axis=1)
        extp = jnp.pad(ext, ((0, 0), (PG - POOL_BUF - Ts, 0), (0, 0))).reshape(Bs * PG, D)
        yb = _pool_ext(extp, l, D, pw16, psr, group_rows=PG, out_row0=PG - Ts, pos0=PAST_LEN)
        yb = yb.reshape(Bs, PG, D)[:, PG - Ts:].reshape(Ms, D)
        y1 = _merge_out(P, l, D, ya, yb, wout16)
        xs, z2 = _resnorm(xs, gain(n2r, l), res=(y1, mod_s, l, 2, 1), mod=(mod_s, l, 4, 3, 1), tm=tms,
                          z_t=True)
        ys = peer(z2)
        outs["wkv_s"].append(s_fin)
        outs["shift_s"].append(shift_out(P, Bs, Ts))
        outs["pool_s"].append(ext[:, -POOL_BUF:])

    fin = (final_g.reshape(1, D), pl.BlockSpec((1, D), lambda i: (0, 0)))
    _, y_prompt = _resnorm(xp, fin, res=(yp, mod_p, L - 1, 5, bpg), z_dtype=F32, tm=tmn, y_t=True)
    _, y_sample = _resnorm(xs, fin, res=(ys, mod_s, L - 1, 5, 1), z_dtype=F32, tm=tms, y_t=True)
    st = lambda name: jnp.stack(outs[name], axis=0)
    return (y_prompt.reshape(Bp, Tp, D), y_sample.reshape(Bs, Ts, D),
            st("wkv_p"), st("shift_p"), st("pool_p"), st("wkv_s"), st("shift_s"), st("pool_s"))
```

```python
import functools

import jax
import jax.numpy as jnp
from jax import lax
from jax.experimental import pallas as pl
from jax.experimental.pallas import tpu as pltpu

F32 = jnp.float32
BF16 = jnp.bfloat16

HEAD_DIM = 64
LANES = 128
MXU_WIDTH = 256
GN_EPS = 64e-5
NORM_EPS = 1e-6
PAST_LEN = 16384
POOL_WINDOWS = (2, 4, 8, 16)
POOL_BUF = 15
PEER_TOPK = 16
N_KEYS = 128
SCAN_ROWS = 64
VMEM_LIMIT = 56 * 1024 * 1024


def _cp(*sem):
    return pltpu.CompilerParams(dimension_semantics=sem, vmem_limit_bytes=VMEM_LIMIT)


def _rup(n, m):
    return (n + m - 1) // m * m


def _dot(a, b):
    return jnp.dot(a, b, preferred_element_type=F32)


def _dot_nt(a, b):
    return lax.dot_general(a, b, (((1,), (1,)), ((), ())), preferred_element_type=F32)


def _lspec(l, block, index_map):
    return pl.BlockSpec((None,) + tuple(block), lambda *g: (l,) + tuple(index_map(*g)))


def _mod_kernel(c_ref, w_ref, b_ref, o_ref):
    c = c_ref[...]
    a = (c * jax.nn.sigmoid(c)).astype(BF16)
    o_ref[...] = _dot(a, w_ref[...].astype(BF16)) + b_ref[...]


def _modulation(c_all, w_mod, b_mod):
    L, D, N = w_mod.shape
    Mc = c_all.shape[0]
    tn = 512
    return pl.pallas_call(
        _mod_kernel,
        grid=(L, N // tn),
        in_specs=[
            pl.BlockSpec((Mc, D), lambda l, j: (0, 0)),
            pl.BlockSpec((None, D, tn), lambda l, j: (l, 0, j)),
            pl.BlockSpec((None, 1, tn), lambda l, j: (l, 0, j)),
        ],
        out_specs=pl.BlockSpec((None, Mc, tn), lambda l, j: (l, 0, j)),
        out_shape=jax.ShapeDtypeStruct((L, Mc, N), F32),
        compiler_params=_cp("parallel", "parallel"),
        name="modulation",
    )(c_all, w_mod, b_mod.reshape(L, 1, N))


def _resnorm_kernel(*refs, has_res, has_mod, y_t, z_t):
    it = iter(refs)
    x_ref = next(it)
    if has_res:
        y_ref = next(it)
        gt_ref = next(it)
    g_ref = next(it)
    if has_mod:
        sc_ref = next(it)
        sh_ref = next(it)
    if has_res:
        xo_ref = next(it)
    z_ref = next(it)
    x = x_ref[...]
    if has_res:
        y = y_ref[...].T if y_t else y_ref[...]
        x = x + gt_ref[...] * y
        xo_ref[...] = x
    ms = jnp.mean(x * x, axis=-1, keepdims=True)
    n = x * lax.rsqrt(ms + NORM_EPS) * g_ref[...]
    if has_mod:
        n = n * (1.0 + sc_ref[...]) + sh_ref[...]
    z_ref[...] = (n.T if z_t else n).astype(z_ref.dtype)


def _resnorm(x, g_spec, *, res=None, mod=None, z_dtype=BF16, tm=256, y_t=False, z_t=False):
    M, D = x.shape
    tm = min(tm, M)
    row = pl.BlockSpec((tm, D), lambda i: (i, 0))
    col = pl.BlockSpec((D, tm), lambda i: (0, i))

    def mspec(m4, l, piece, bpg):
        return pl.BlockSpec((None, None, m4.shape[2], D), lambda i: (l, i // bpg, 0, piece))

    args, specs = [x], [row]
    if res is not None:
        y, m4, l, piece, bpg = res
        args += [y, m4]
        specs += [col if y_t else row, mspec(m4, l, piece, bpg)]
    args.append(g_spec[0])
    specs.append(g_spec[1])
    if mod is not None:
        m4, l, scp, shp, bpg = mod
        args += [m4, m4]
        specs += [mspec(m4, l, scp, bpg), mspec(m4, l, shp, bpg)]
    out_shape, out_specs = [], []
    if res is not None:
        out_shape.append(jax.ShapeDtypeStruct((M, D), F32))
        out_specs.append(row)
    out_shape.append(jax.ShapeDtypeStruct((D, M) if z_t else (M, D), z_dtype))
    out_specs.append(col if z_t else row)
    outs = pl.pallas_call(
        functools.partial(_resnorm_kernel, has_res=res is not None, has_mod=mod is not None, y_t=y_t, z_t=z_t),
        grid=(M // tm,),
        in_specs=specs,
        out_specs=out_specs,
        out_shape=out_shape,
        compiler_params=_cp("parallel"),
        name="resnorm",
    )(*args)
    return outs if res is not None else (None, outs[0])


def _mm_kernel(a_ref, b_ref, o_ref):
    o_ref[...] = _dot(a_ref[...], b_ref[...]).astype(o_ref.dtype)


def _mm(a, b, *, la=None, lb=None, tm=1024, tn=512, out_dtype=F32):
    M, K = a.shape[-2:]
    N = b.shape[-1]
    tm, tn = min(tm, M), min(tn, N)
    a_spec = (pl.BlockSpec((tm, K), lambda i, j: (i, 0)) if la is None
              else _lspec(la, (tm, K), lambda i, j: (i, 0)))
    b_spec = (pl.BlockSpec((K, tn), lambda i, j: (0, j)) if lb is None
              else _lspec(lb, (K, tn), lambda i, j: (0, j)))
    return pl.pallas_call(
        _mm_kernel,
        grid=(M // tm, N // tn),
        in_specs=[a_spec, b_spec],
        out_specs=pl.BlockSpec((tm, tn), lambda i, j: (i, j)),
        out_shape=jax.ShapeDtypeStruct((M, N), out_dtype),
        compiler_params=_cp("parallel", "parallel"),
        name="mm",
    )(a, b)


def _softplus(u):
    return jnp.maximum(u, 0.0) + jnp.log1p(jnp.exp(-jnp.abs(u)))


def _prep_kernel(*refs, seq_len, per_row_state, lw, la):
    (r_ref, k_ref, v_ref, l_ref, hr_ref, hk_ref, hv_ref, hl_ref) = refs[:8]
    n = 8
    if per_row_state:
        sr_ref, sk_ref, sv_ref, sl_ref = refs[n:n + 4]
        n += 4
    (mur_ref, muk_ref, muv_ref, mul_ref, w0_ref, a0_ref, wup_ref, aup_ref, gup_ref) = refs[n:n + 9]
    ro_ref, ko_ref, vo_ref, lwo_ref, ao_ref, go_ref = refs[n + 9:]
    tm = r_ref.shape[0]
    base = pl.program_id(0) * tm
    row = lax.broadcasted_iota(jnp.int32, (tm, 1), 0)
    pos = (base + row) % seq_len
    first = row == 0
    start = pos == 0

    def mixed(x_ref, h_ref, s_ref, mu_ref):
        x = x_ref[...]
        prev = jnp.where(first, h_ref[7:8, :], pltpu.roll(x, 1, 0))
        if per_row_state:
            prev = jnp.where(start, s_ref[...], prev)
        else:
            prev = jnp.where(start, 0.0, prev)
        return x + mu_ref[...] * (prev - x)

    sr = sk = sv = sl = None
    if per_row_state:
        sr, sk, sv, sl = sr_ref, sk_ref, sv_ref, sl_ref
    r = mixed(r_ref, hr_ref, sr, mur_ref)
    k = mixed(k_ref, hk_ref, sk, muk_ref)
    v = mixed(v_ref, hv_ref, sv, muv_ref)
    lo = mixed(l_ref, hl_ref, sl, mul_ref)
    wd = lo[:, :lw]
    ad = lo[:, lw:lw + la]
    gd = lo[:, lw + la:]
    wlin = w0_ref[...] + _dot(jnp.tanh(wd).astype(BF16), wup_ref[...])
    w_log = -_softplus(-wlin) - 0.5
    ro_ref[...] = r
    ko_ref[...] = k
    vo_ref[...] = v
    lwo_ref[...] = -jnp.exp(w_log)
    ao_ref[...] = jax.nn.sigmoid(a0_ref[...] + _dot(ad.astype(BF16), aup_ref[...]))
    go_ref[...] = _dot(jax.nn.sigmoid(gd).astype(BF16), gup_ref[...])


def _rwkv_prep(P, l, D, LB, lw, la, seq_len, state_rows, mu_rkv, mu_l, w0, a0, wup, aup, gup, tm=128):
    M = P.shape[0]
    tm = min(tm, M)
    lcol = 6 * D // LB
    per_row_state = state_rows is not None

    def cur(w, c):
        return pl.BlockSpec((tm, w), lambda i: (i, c))

    def halo(w, c):
        return pl.BlockSpec((8, w), lambda i: (jnp.maximum(i * (tm // 8) - 1, 0), c))

    args = [P] * 8
    specs = [cur(D, 0), cur(D, 1), cur(D, 2), cur(LB, lcol),
             halo(D, 0), halo(D, 1), halo(D, 2), halo(LB, lcol)]
    if per_row_state:
        s_rkv, s_l = state_rows
        args += [s_rkv, s_rkv, s_rkv, s_l]
        specs += [_lspec(l, (tm, D), lambda i, c=c: (i, c)) for c in range(3)]
        specs += [_lspec(l, (tm, LB), lambda i: (i, 0))]

    def const(arr, c=0, w=None):
        w = arr.shape[-1] if w is None else w
        return _lspec(l, (arr.shape[-2], w), lambda i: (0, c))

    args += [mu_rkv, mu_rkv, mu_rkv, mu_l, w0, a0, wup, aup, gup]
    specs += [const(mu_rkv, 0, D), const(mu_rkv, 1, D), const(mu_rkv, 2, D), const(mu_l),
              const(w0), const(a0), const(wup), const(aup), const(gup)]
    out = jax.ShapeDtypeStruct((M, D), F32)
    return pl.pallas_call(
        functools.partial(_prep_kernel, seq_len=seq_len, per_row_state=per_row_state, lw=lw, la=la),
        grid=(M // tm,),
        in_specs=specs,
        out_specs=[pl.BlockSpec((tm, D), lambda i: (i, 0))] * 6,
        out_shape=[out] * 6,
        compiler_params=_cp("parallel"),
        name="rwkv_prep",
    )(*args)


def _split3(x):
    h = x.astype(BF16)
    r1 = x - h.astype(F32)
    m = r1.astype(BF16)
    lo = (r1 - m.astype(F32)).astype(BF16)
    return h, m, lo


def _scan_kernel(*refs, G, Tc, zero_init):
    (r_ref, k_ref, v_ref, lw_ref, a_ref, g_ref, kk_ref, ka_ref, rk_ref, lnw_ref, lnb_ref) = refs[:11]
    s0_ref = None if zero_init else refs[11]
    y_ref, so_ref, st_ref = refs[-3:]
    C = SCAN_ROWS
    nseq = C // Tc
    Q = range(G)

    @pl.when(pl.program_id(2) == 0)
    def _():
        if zero_init:
            st_ref[...] = jnp.zeros_like(st_ref)
        else:
            z = jnp.zeros((HEAD_DIM, HEAD_DIM), F32)
            for s in range(nseq):
                for q in Q:
                    top = jnp.concatenate([s0_ref[s, 2 * q], z], axis=1)
                    bot = jnp.concatenate([z, s0_ref[s, 2 * q + 1]], axis=1)
                    st_ref[s, q] = jnp.concatenate([top, bot], axis=0)

    lane = lax.broadcasted_iota(jnp.int32, (1, LANES), 1)
    m_a = lane < HEAD_DIM
    fa = m_a.astype(F32)
    fb = 1.0 - fa
    ri = lax.broadcasted_iota(jnp.int32, (C, LANES), 0)
    cj = lax.broadcasted_iota(jnp.int32, (C, LANES), 1)
    sj = jnp.where(cj >= C, cj - C, cj)
    far = 1 << 20
    sj = jnp.where((ri // Tc) == (sj // Tc), sj, far)
    strict = sj < ri
    incl = sj <= ri
    col_a = (cj < C).astype(F32)
    col_b = 1.0 - col_a
    ti = lax.broadcasted_iota(jnp.int32, (C, 3 * C), 0)
    tj = lax.broadcasted_iota(jnp.int32, (C, 3 * C), 1) % C
    tri3 = jnp.where(jnp.where((ti // Tc) == (tj // Tc), tj, far) <= ti, 1.0, 0.0).astype(BF16)
    bi = lax.broadcasted_iota(jnp.int32, (LANES, LANES), 0)
    bj = lax.broadcasted_iota(jnp.int32, (LANES, LANES), 1)
    bd = (bi < HEAD_DIM) == (bj < HEAD_DIM)
    eye = (bi == bj).astype(F32)
    eye16 = eye.astype(BF16)
    levels = Tc.bit_length() - 1

    def head_sum(x):
        sa = jnp.sum(x * fa, axis=1, keepdims=True)
        sb = jnp.sum(x * fb, axis=1, keepdims=True)
        return jnp.where(m_a, sa, sb)

    def stack(x):
        return jnp.concatenate([x * fa, x * fb], axis=0)

    def seq_rows(x, s):
        return x[s * Tc:(s + 1) * Tc]

    sls = [slice(q * LANES, (q + 1) * LANES) for q in Q]
    r = [r_ref[:, sl] for sl in sls]
    k = [k_ref[:, sl] for sl in sls]
    v = [v_ref[:, sl] for sl in sls]
    lw = [lw_ref[:, sl] for sl in sls]
    a = [a_ref[:, sl] for sl in sls]
    states = [[st_ref[s, q] for s in range(nseq)] for q in Q]
    cum = [_dot(tri3, jnp.concatenate(_split3(lw[q]), axis=0)) for q in Q]
    kk = [k[q] * kk_ref[:, sls[q]] for q in Q]
    kk = [kk[q] / jnp.maximum(jnp.sqrt(head_sum(kk[q] * kk[q])), 1e-12) for q in Q]
    km = [k[q] * (1.0 + (a[q] - 1.0) * ka_ref[:, sls[q]]) for q in Q]
    wcum = [jnp.exp(cum[q]) for q in Q]
    winv = [jnp.exp(-cum[q]) for q in Q]
    rt = [r[q] * wcum[q] for q in Q]
    at = [-kk[q] * jnp.exp(cum[q] - lw[q]) for q in Q]
    bt = [kk[q] * a[q] * winv[q] for q in Q]
    kt = [km[q] * winv[q] for q in Q]
    x16 = [jnp.concatenate([at[q], rt[q]], axis=0).astype(BF16) for q in Q]
    vst = [stack(v[q]).astype(BF16) for q in Q]
    amat = [_dot_nt(x16[q], jnp.concatenate([stack(bt[q]), stack(kt[q])], axis=0).astype(BF16)) for q in Q]
    if nseq == 1:
        xs = [_dot_nt(x16[q], states[q][0].astype(BF16)) for q in Q]
        xs_a = [xs[q][:C] for q in Q]
        xs_r = [xs[q][C:] for q in Q]
    else:
        parts = [[_dot_nt(jnp.concatenate([seq_rows(at[q], s), seq_rows(rt[q], s)], axis=0).astype(BF16),
                          states[q][s].astype(BF16)) for s in range(nseq)] for q in Q]
        xs_a = [jnp.concatenate([p_[:Tc] for p_ in parts[q]], axis=0) for q in Q]
        xs_r = [jnp.concatenate([p_[Tc:] for p_ in parts[q]], axis=0) for q in Q]
    a_ab = [jnp.where(strict, amat[q][:C, :LANES], 0.0) for q in Q]
    a_ak = [jnp.where(strict, amat[q][:C, LANES:], 0.0) for q in Q]
    a_rb = [jnp.where(incl, amat[q][C:, :LANES], 0.0) for q in Q]
    a_rk = [jnp.where(incl, amat[q][C:, LANES:], 0.0) for q in Q]
    rhs = [xs_a[q] + _dot(a_ak[q].astype(BF16), vst[q]) for q in Q]
    nbd = [jnp.concatenate([a_ab[q] * col_a, a_ab[q] * col_b], axis=0) for q in Q]
    tinv = [eye + nbd[q] for q in Q]
    if levels > 1:
        n16 = [nbd[q].astype(BF16) for q in Q]
        pw = [_dot(n16[q], n16[q]) for q in Q]
        for lev in range(1, levels):
            p16 = [pw[q].astype(BF16) for q in Q]
            if lev + 1 < levels:
                pt = [_dot(p16[q], jnp.concatenate([p16[q], tinv[q].astype(BF16)], axis=1)) for q in Q]
                pw = [pt[q][:, :LANES] for q in Q]
                tinv = [tinv[q] + pt[q][:, LANES:] for q in Q]
            else:
                tinv = [tinv[q] + _dot(p16[q], tinv[q].astype(BF16)) for q in Q]
    ust = [_dot(tinv[q].astype(BF16), stack(rhs[q]).astype(BF16)) for q in Q]
    u = [ust[q][:C] + ust[q][C:] for q in Q]
    y = [xs_r[q] + _dot(jnp.concatenate([a_rb[q], a_rk[q]], axis=1).astype(BF16),
                        jnp.concatenate([ust[q].astype(BF16), vst[q]], axis=0)) for q in Q]
    uvt = [[_dot_nt(eye16, jnp.concatenate([seq_rows(u[q], s), seq_rows(v[q], s)], axis=0).astype(BF16))
            for s in range(nseq)] for q in Q]
    upd = [[_dot(uvt[q][s].astype(BF16),
                 jnp.concatenate([seq_rows(bt[q], s), seq_rows(kt[q], s)], axis=0).astype(BF16))
            for s in range(nseq)] for q in Q]
    for q in Q:
        for s in range(nseq):
            st_ref[s, q] = ((states[q][s] + jnp.where(bd, upd[q][s], 0.0))
                            * wcum[q][(s + 1) * Tc - 1:(s + 1) * Tc, :])
    for q in Q:
        sl = sls[q]
        mean = head_sum(y[q]) * (1.0 / HEAD_DIM)
        d = y[q] - mean
        var = head_sum(d * d) * (1.0 / HEAD_DIM)
        yn = d * lax.rsqrt(var + GN_EPS) * lnw_ref[:, sl] + lnb_ref[:, sl]
        bonus = head_sum(r[q] * km[q] * rk_ref[:, sl]) * v[q]
        y_ref[:, sl] = (yn + bonus) * g_ref[:, sl]

    @pl.when(pl.program_id(2) == pl.num_programs(2) - 1)
    def _():
        for s in range(nseq):
            for q in Q:
                full = st_ref[s, q]
                so_ref[s, 2 * q] = full[:HEAD_DIM, :HEAD_DIM]
                so_ref[s, 2 * q + 1] = full[HEAD_DIM:, HEAD_DIM:]


def _rwkv_scan(l, L, r, k, v, lw, a, g, k_k, k_a, r_k, ln_w, ln_b, s0, states_all, nstates, Tseq, Tc, G=16):
    M, D = r.shape
    C = SCAN_ROWS
    nseq = C // Tc
    nc = Tseq // Tc
    assert 2 * C == LANES and (nc == 1 or nseq == 1)
    G = min(G, D // LANES)
    H = D // HEAD_DIM
    tok = pl.BlockSpec((C, G * LANES), lambda b, p, c: (b * nc + c, p))
    par = _lspec(l, (1, G * LANES), lambda b, p, c: (0, p))
    st_blk = _lspec(l, (nseq, 2 * G, HEAD_DIM, HEAD_DIM), lambda b, p, c: (b, p, 0, 0))
    args = [r, k, v, lw, a, g, k_k, k_a, r_k, ln_w, ln_b]
    specs = [tok] * 6 + [par] * 5
    if s0 is not None:
        args.append(s0)
        specs.append(st_blk)
    aliases = {}
    if states_all is not None:
        aliases[len(args)] = 1
        args.append(states_all)
        specs.append(pl.BlockSpec(memory_space=pl.ANY))
    return pl.pallas_call(
        functools.partial(_scan_kernel, G=G, Tc=Tc, zero_init=s0 is None),
        grid=(M // (C * nc), D // (G * LANES), nc),
        in_specs=specs,
        out_specs=[tok, st_blk],
        out_shape=[jax.ShapeDtypeStruct((M, D), F32),
                   jax.ShapeDtypeStruct((L, nstates, H, HEAD_DIM, HEAD_DIM), F32)],
        scratch_shapes=[pltpu.VMEM((nseq, G, LANES, LANES), F32)],
        input_output_aliases=aliases,
        compiler_params=_cp("parallel", "parallel", "arbitrary"),
        name="rwkv_scan",
    )(*args)


def _pool_kernel(*refs, has_halo, seq_len, group_rows, out_row0, pos0, gd):
    if has_halo:
        p_ref, h_ref, pw_ref, ps_ref, o_ref = refs
    else:
        p_ref, pw_ref, ps_ref, o_ref = refs
    tm = p_ref.shape[0]
    if has_halo:
        base = (pl.program_id(0) * tm) % seq_len
        halo = jnp.where(base == 0, 0.0, h_ref[...])
        ext = jnp.concatenate([halo, p_ref[...]], axis=0)
        off = h_ref.shape[0]
        row = lax.broadcasted_iota(jnp.int32, (tm, 1), 0)
        pos = base + row + pos0
    else:
        ext = p_ref[...]
        off = 0
        row = lax.broadcasted_iota(jnp.int32, (tm, 1), 0)
        pos = pos0 + (row % group_rows) - out_row0
    for gi, w in enumerate(POOL_WINDOWS):
        x = ext[:, gi * gd:(gi + 1) * gd]
        s = x
        span = 1
        while span < w:
            s = s + pltpu.roll(s, span, 0)
            span *= 2
        cnt = jnp.minimum(w, pos + 1).astype(F32)
        mixed = s[off:] / cnt - x[off:]
        y = _dot(mixed.astype(BF16), pw_ref[gi])
        o_ref[:, gi * gd:(gi + 1) * gd] = y * ps_ref[:, gi * gd:(gi + 1) * gd]


def _pool(P, l, pcol, D, pw16, ps, *, seq_len, tm=256):
    M = P.shape[0]
    tm = min(tm, seq_len)
    G = len(POOL_WINDOWS)
    gd = D // G
    return pl.pallas_call(
        functools.partial(_pool_kernel, has_halo=True, seq_len=seq_len, group_rows=1, out_row0=0,
                          pos0=0, gd=gd),
        grid=(M // tm,),
        in_specs=[pl.BlockSpec((tm, D), lambda i: (i, pcol)),
                  pl.BlockSpec((16, D), lambda i: (jnp.maximum(i * (tm // 16) - 1, 0), pcol)),
                  _lspec(l, (G, gd, gd), lambda i: (0, 0, 0)),
                  _lspec(l, (1, D), lambda i: (0, 0))],
        out_specs=pl.BlockSpec((tm, D), lambda i: (i, 0)),
        out_shape=jax.ShapeDtypeStruct((M, D), F32),
        compiler_params=_cp("parallel"),
        name="pool",
    )(P, P, pw16, ps)


def _pool_ext(ext, l, D, pw16, ps, *, group_rows, out_row0, pos0, tm=384):
    M = ext.shape[0]
    tm = min(tm, M)
    G = len(POOL_WINDOWS)
    gd = D // G
    return pl.pallas_call(
        functools.partial(_pool_kernel, has_halo=False, seq_len=1, group_rows=group_rows,
                          out_row0=out_row0, pos0=pos0, gd=gd),
        grid=(M // tm,),
        in_specs=[pl.BlockSpec((tm, D), lambda i: (i, 0)),
                  _lspec(l, (G, gd, gd), lambda i: (0, 0, 0)),
                  _lspec(l, (1, D), lambda i: (0, 0))],
        out_specs=pl.BlockSpec((tm, D), lambda i: (i, 0)),
        out_shape=jax.ShapeDtypeStruct((M, D), F32),
        compiler_params=_cp("parallel"),
        name="pool_ext",
    )(ext, pw16, ps)


def _merge_out_kernel(ga_ref, gb_ref, ya_ref, yb_ref, w_ref, o_ref):
    m = jax.nn.sigmoid(ga_ref[...]) * ya_ref[...] + jax.nn.sigmoid(gb_ref[...]) * yb_ref[...]
    o_ref[...] = _dot(m.astype(BF16), w_ref[...])


def _merge_out(P, l, D, ya, yb, wout16, tm=256):
    M = P.shape[0]
    tm = min(tm, M)
    tok = pl.BlockSpec((tm, D), lambda i: (i, 0))
    return pl.pallas_call(
        _merge_out_kernel,
        grid=(M // tm,),
        in_specs=[pl.BlockSpec((tm, D), lambda i: (i, 4)),
                  pl.BlockSpec((tm, D), lambda i: (i, 5)),
                  tok, tok,
                  _lspec(l, (D, D), lambda i: (0, 0))],
        out_specs=tok,
        out_shape=jax.ShapeDtypeStruct((M, D), F32),
        compiler_params=_cp("parallel"),
        name="merge_out",
    )(P, P, ya, yb, wout16)


PEER_RANKS = PEER_TOPK + 1
PEER_RANK_ROWS = _rup(PEER_RANKS, 8)
PEER_PAIR_COUNTS = tuple(min(PEER_RANKS, PEER_RANKS // (i + 1)) for i in range(PEER_RANKS))
PEER_CAND_ROWS = _rup(sum(PEER_PAIR_COUNTS), 8)


def _peer_select_kernel(qt_ref, keys_ref, ethr_ref, e1_ref, e2_ref,
                        s1_scr, s2_ref, t1_scr, t2_scr, cand_scr, top_scr, *, heads):
    half = N_KEYS
    tl = qt_ref.shape[1]
    neg = -jnp.inf
    for h in range(heads):
        q1 = qt_ref[2 * h * half:(2 * h + 1) * half, :].astype(BF16)
        q2 = qt_ref[(2 * h + 1) * half:(2 * h + 2) * half, :].astype(BF16)
        s1_scr[h] = _dot(keys_ref[0], q1)
        s2_ref[h] = _dot(keys_ref[1], q2)

    def next_max(x, prev):
        return jnp.max(jnp.where(x < prev, x, neg), axis=0, keepdims=True)

    def rank_body(i, prev):
        new = []
        for h in range(heads):
            m1 = next_max(s1_scr[h], prev[2 * h])
            m2 = next_max(s2_ref[h], prev[2 * h + 1])
            t1_scr[h, pl.ds(i, 1), :] = m1
            t2_scr[h, pl.ds(i, 1), :] = m2
            new += [m1, m2]
        return tuple(new)

    inf_row = jnp.full((1, tl), jnp.inf, F32)
    lax.fori_loop(0, PEER_RANKS, rank_body, (inf_row,) * (2 * heads))

    for h in range(heads):
        cand_scr[h, PEER_CAND_ROWS - 8:, :] = jnp.full((8, tl), neg, F32)
        r0 = 0
        for i, n in enumerate(PEER_PAIR_COUNTS):
            cand_scr[h, r0:r0 + n, :] = t1_scr[h, i:i + 1, :] + t2_scr[h, 0:n, :]
            r0 += n

    def cand_body(i, prev):
        new = []
        for h in range(heads):
            m = next_max(cand_scr[h], prev[h])
            top_scr[h, pl.ds(i, 1), :] = m
            new.append(m)
        return tuple(new)

    lax.fori_loop(0, PEER_RANKS, cand_body, (inf_row,) * heads)

    for h in range(heads):
        top = top_scr[h]
        t16 = top[PEER_TOPK - 1:PEER_TOPK]
        t17 = top[PEER_TOPK:PEER_TOPK + 1]
        mid = jnp.where(t17 > neg, 0.5 * (t16 + t17), t16)
        z = jnp.sum(jnp.exp(top[:PEER_TOPK] - top[0:1]), axis=0, keepdims=True)
        s1 = s1_scr[h]
        m2 = t2_scr[h, 0:1, :]
        ethr_ref[h] = jnp.exp((mid - m2) - s1)
        e1_ref[h] = jnp.exp(s1 - t1_scr[h, 0:1, :]) / z
        e2_ref[h] = jnp.exp(s2_ref[h] - m2)


def _peer_select(qt, keys16, l, heads, tl=128):
    QD, M = qt.shape
    tl = min(tl, M)
    big = jax.ShapeDtypeStruct((heads, N_KEYS, M), F32)
    bspec = pl.BlockSpec((heads, N_KEYS, tl), lambda i: (0, 0, i))
    ranks = pltpu.VMEM((heads, PEER_RANK_ROWS, tl), F32)
    return pl.pallas_call(
        functools.partial(_peer_select_kernel, heads=heads),
        grid=(M // tl,),
        in_specs=[pl.BlockSpec((QD, tl), lambda i: (0, i)),
                  _lspec(l, keys16.shape[1:], lambda i: (0, 0, 0))],
        out_specs=[bspec] * 3,
        out_shape=[big] * 3,
        scratch_shapes=[pltpu.VMEM((heads, N_KEYS, tl), F32), pltpu.VMEM((heads, N_KEYS, tl), F32), ranks, ranks,
                        pltpu.VMEM((heads, PEER_CAND_ROWS, tl), F32), ranks],
        compiler_params=_cp("parallel"),
        name="peer_select",
    )(qt, keys16)


PEER_LANE_TILE = 128
PEER_STEP_KEYS = 4
PEER_UNIT_KEYS = 1
PEER_UNIT_GROUPS = 16


def _gelu_tanh(x):
    return 0.5 * x * (1.0 + jnp.tanh(0.7978845608028654 * (x + 0.044715 * (x * x * x))))


def _peer_dense_kernel(zt_ref, u_ref, vt_ref, thr_ref, e1_ref, e2_ref, o_ref, ht0, ht1, wt0, wt1, *, heads, nblk):
    s = pl.program_id(1)
    tm = zt_ref.shape[1]
    lt = min(PEER_LANE_TILE, tm)
    ka = PEER_STEP_KEYS
    uk = PEER_UNIT_KEYS
    groups = N_KEYS // 8
    mxu_n = min(MXU_WIDTH, tm)
    n_lane = tm // mxu_n
    d_out = o_ref.shape[0]
    ug = PEER_UNIT_GROUPS
    units = [(t0, ap, g0) for t0 in range(0, tm, lt) for ap in range(0, ka, uk) for g0 in range(0, groups, ug)]
    nu = len(units)

    @pl.when(s == 0)
    def _():
        o_ref[...] = jnp.zeros_like(o_ref)
        for buf in (ht0, ht1, wt0, wt1):
            buf[...] = jnp.zeros_like(buf)

    valid = jnp.logical_and(s >= 1, s <= nblk)

    def order_token(res):
        bits = pltpu.bitcast(res[:8, :lt], jnp.int32)
        bits = lax.shift_right_logical(lax.shift_right_logical(bits, 16), 16)
        return bits.astype(F32)

    def body(ht_new, ht_old, wt_new, wt_old, k0):
        n_sc = 2 * n_lane
        sc_rows = ka * N_KEYS // 2
        n_cb = 4 * n_lane
        ob = d_out // (n_cb // n_lane)

        def scores(piece):
            rs = slice((piece // n_lane) * sc_rows, (piece // n_lane + 1) * sc_rows)
            ts = slice((piece % n_lane) * mxu_n, (piece % n_lane + 1) * mxu_n)
            res = _dot(u_ref[rs, :], zt_ref[:, ts])
            ht_new[rs, ts] = res
            return order_token(res)

        def combine(piece):
            os_ = slice((piece // n_lane) * ob, (piece // n_lane + 1) * ob)
            ts = slice((piece % n_lane) * mxu_n, (piece % n_lane + 1) * mxu_n)
            res = _dot(vt_ref[os_, :], wt_new[:, ts])
            o_ref[os_, ts] += res
            return order_token(res)

        def weights(unit, start):
            t0, ap, g0 = units[unit]
            ts = slice(t0, t0 + lt)
            accs = [[start for _ in range(ug)] for _ in range(uk)]
            for h in range(heads):
                thr_b = [jnp.broadcast_to(thr_ref[h, k0 + ap + d:k0 + ap + d + 1, ts], (8, lt)) for d in range(uk)]
                e1_b = [jnp.broadcast_to(e1_ref[h, k0 + ap + d:k0 + ap + d + 1, ts], (8, lt)) for d in range(uk)]
                for g in range(ug):
                    e2t = e2_ref[h, (g0 + g) * 8:(g0 + g + 1) * 8, ts]
                    for d in range(uk):
                        accs[d][g] = accs[d][g] + jnp.where(e2t >= thr_b[d], e1_b[d] * e2t, 0.0)
            for d in range(uk):
                for g in range(0, ug, 2):
                    rs = slice((ap + d) * N_KEYS + (g0 + g) * 8, (ap + d) * N_KEYS + (g0 + g + 2) * 8)
                    w = jnp.concatenate([accs[d][g], accs[d][g + 1]], axis=0) * _gelu_tanh(ht_old[rs, ts])
                    wt_old[rs, ts] = jnp.where(valid, w, 0.0).astype(BF16)

        zero = jnp.zeros((8, lt), F32)
        pending, ready = [], []
        sc_every, cb_every = max(nu // n_sc, 1), max(nu // n_cb, 1)
        for unit in range(nu):
            if unit % cb_every == 0 and unit // cb_every < n_cb:
                pending.append(combine(unit // cb_every))
            if unit % sc_every == 0 and unit // sc_every < n_sc:
                pending.append(scores(unit // sc_every))
            start = zero
            for tok in ready:
                start = start + tok
            ready, pending = pending, []
            weights(unit, start)

    @pl.when(s % 2 == 0)
    def _():
        body(ht0, ht1, wt0, wt1, (8 - ka) % 8)

    @pl.when(s % 2 == 1)
    def _():
        body(ht1, ht0, wt1, wt0, 0)


def _peer_dense(zt, u16, vt16, l, ethr, e1, e2, tm=1024):
    D, M = zt.shape
    E = u16.shape[1]
    heads = e2.shape[0]
    tm = min(tm, M)
    ka = PEER_STEP_KEYS
    ec = ka * N_KEYS
    nblk = E // ec
    blk = lambda s_, back: jnp.clip(s_ - back, 0, nblk - 1)
    assert ka in (4, 8)
    rows_prev = pl.BlockSpec((heads, 8, tm), lambda i, s_: (0, blk(s_, 1) * ka // 8, i))
    buf_f32 = pltpu.VMEM((ec, tm), F32)
    buf_b16 = pltpu.VMEM((ec, tm), BF16)
    return pl.pallas_call(
        functools.partial(_peer_dense_kernel, heads=heads, nblk=nblk),
        grid=(M // tm, nblk + 2),
        in_specs=[pl.BlockSpec((D, tm), lambda i, s_: (0, i)),
                  _lspec(l, (ec, D), lambda i, s_: (blk(s_, 0), 0)),
                  _lspec(l, (D, ec), lambda i, s_: (0, blk(s_, 2))),
                  rows_prev, rows_prev,
                  pl.BlockSpec((heads, N_KEYS, tm), lambda i, s_: (0, 0, i))],
        out_specs=pl.BlockSpec((D, tm), lambda i, s_: (0, i)),
        out_shape=jax.ShapeDtypeStruct((D, M), F32),
        scratch_shapes=[buf_f32, buf_f32, buf_b16, buf_b16],
        compiler_params=_cp("parallel", "arbitrary"),
        name="peer_dense",
    )(zt, u16, vt16, ethr, e1, e2)


def _pad_last(w, n):
    return jnp.pad(w, [(0, 0)] * (w.ndim - 1) + [(0, n - w.shape[-1])])


def _pad_rows(w, n):
    return jnp.pad(w, [(0, 0)] * (w.ndim - 2) + [(0, n - w.shape[-2]), (0, 0)])


def kernel(x_prompt, x_sample, c_prompt, c_sample, state_wkv, state_shift, state_pool, norm1_g, norm2_g, final_g, w_mod, b_mod, w_in, mu_shift, w0, w_up, a0, a_up, g_up, k_k, k_a, r_k, ln_w, ln_b, pool_w, pool_scale, w_out, w_q, sub_keys, peer_u, peer_v):
    Bp, Tp, D = x_prompt.shape
    Bs, Ts, _ = x_sample.shape
    L = w_in.shape[0]
    H = D // HEAD_DIM
    dw, da, dg = w_up.shape[1], a_up.shape[1], g_up.shape[1]
    LW, LA, LG = _rup(dw, LANES), _rup(da, LANES), _rup(dg, LANES)
    LB = LW + LA + LG
    o3 = 3 * D
    o4, o5, o6 = o3 + dw, o3 + dw + da, o3 + dw + da + dg
    peer_heads = w_q.shape[2] // (2 * N_KEYS)
    Mp, Ms = Bp * Tp, Bs * Ts
    TS8 = 8
    PG = 24
    assert Tp % SCAN_ROWS == 0 and Ts <= TS8 and POOL_BUF + Ts <= PG

    Mc = _rup(Bp + Bs, 8)
    c_all = jnp.pad(jnp.concatenate([c_prompt, c_sample], axis=0), ((0, Mc - Bp - Bs), (0, 0)))
    mod_all = _modulation(c_all, w_mod, b_mod)
    tmn, tms = min(256, Tp), min(256, Ms)
    bpg = Tp // tmn
    mod_p = mod_all[:, :Bp].reshape(L, Bp, 1, 6 * D)
    mod_s = jnp.repeat(mod_all[:, Bp:Bp + Bs], Ts, axis=1).reshape(L, Ms // tms, tms, 6 * D)

    def repack_cols(t):
        return t[..., :o3], jnp.concatenate(
            [_pad_last(t[..., o3:o4], LW), _pad_last(t[..., o4:o5], LA), _pad_last(t[..., o5:o6], LG)], axis=-1)

    w_rkv, w_l = repack_cols(w_in)
    winp = jnp.concatenate([w_rkv, w_in[..., o6:], w_l], axis=-1).astype(BF16)
    mu_rkv, mu_l = repack_cols(mu_shift[:, None, :])
    wup = _pad_rows(w_up, LW).astype(BF16)
    aup = _pad_rows(a_up, LA).astype(BF16)
    gup = _pad_rows(g_up, LG).astype(BF16)
    rows = lambda t: t.reshape(L, 1, D)
    w0r, a0r, kkr, kar, rkr, lnwr, lnbr, psr = (rows(t) for t in (w0, a0, k_k, k_a, r_k, ln_w, ln_b, pool_scale))
    n1r, n2r = rows(norm1_g), rows(norm2_g)
    pw16 = pool_w.astype(BF16)
    wout16 = w_out.astype(BF16)
    wqt16 = jnp.swapaxes(w_q, 1, 2).astype(BF16)
    keys16 = sub_keys.astype(BF16)
    u16 = peer_u.astype(BF16)
    vt16 = jnp.swapaxes(peer_v, 1, 2).astype(BF16)
    gain = lambda arr, l: (arr, _lspec(l, (1, D), lambda i: (0, 0)))
    st_rkv, st_l = repack_cols(state_shift)
    expand = lambda t: jnp.pad(t[:, :, None, :], ((0, 0), (0, 0), (0, Ts - 1), (0, 0))).reshape(L, Ms, -1)
    st_rkv, st_l = expand(st_rkv), expand(st_l)

    def shift_out(P, nseq, T):
        last = P.reshape(nseq, T, -1)[:, -1]
        return jnp.concatenate(
            [last[:, :o3], last[:, 6 * D:6 * D + dw], last[:, 6 * D + LW:6 * D + LW + da],
             last[:, 6 * D + LW + LA:6 * D + LW + LA + dg]], axis=-1)

    xp = x_prompt.reshape(Mp, D)
    xs = x_sample.reshape(Ms, D)
    yp = ys = None
    outs = {k_: [] for k_ in ("shift_p", "pool_p", "shift_s", "pool_s")}
    wkv_p = jnp.zeros((L, Bp, H, HEAD_DIM, HEAD_DIM), F32)
    wkv_s = jnp.zeros((L, Bs, H, HEAD_DIM, HEAD_DIM), F32)

    for l in range(L):
        def half_layer_1(x, y_prev, mod4, bpg_, tm_):
            res = None if y_prev is None else (y_prev, mod4, l - 1, 5, bpg_)
            x_new, z = _resnorm(x, gain(n1r, l), res=res, mod=(mod4, l, 1, 0, bpg_), tm=tm_, y_t=True)
            return (x if res is None else x_new), _mm(z, winp, lb=l)

        def peer(zt):
            qt = _mm(wqt16, zt, la=l, tm=512, tn=512)
            ethr, e1, e2 = _peer_select(qt, keys16, l, peer_heads)
            return _peer_dense(zt, u16, vt16, l, ethr, e1, e2)

        scan_par = (kkr, kar, rkr, lnwr, lnbr)

        xp, P = half_layer_1(xp, yp, mod_p, bpg, tmn)
        r, k, v, lw, a, g = _rwkv_prep(P, l, D, LB, LW, LA, Tp, None, mu_rkv, mu_l, w0r, a0r, wup, aup, gup)
        ya, wkv_p = _rwkv_scan(l, L, r, k, v, lw, a, g, *scan_par, None, wkv_p, Bp, Tp, SCAN_ROWS)
        yb = _pool(P, l, 3, D, pw16, psr, seq_len=Tp)
        y1 = _merge_out(P, l, D, ya, yb, wout16)
        xp, z2 = _resnorm(xp, gain(n2r, l), res=(y1, mod_p, l, 2, bpg), mod=(mod_p, l, 4, 3, bpg), tm=tmn,
                          z_t=True)
        yp = peer(z2)
        outs["shift_p"].append(shift_out(P, Bp, Tp))
        outs["pool_p"].append(P.reshape(Bp, Tp, -1)[:, Tp - POOL_BUF:, o3:o3 + D])

        xs, P = half_layer_1(xs, ys, mod_s, 1, tms)
        r, k, v, lw, a, g = _rwkv_prep(P, l, D, LB, LW, LA, Ts, (st_rkv, st_l), mu_rkv, mu_l,
                                       w0r, a0r, wup, aup, gup)
        pad8 = lambda t: jnp.pad(t.reshape(Bs, Ts, D), ((0, 0), (0, TS8 - Ts), (0, 0))).reshape(Bs * TS8, D)
        ya8, wkv_s = _rwkv_scan(l, L, pad8(r), pad8(k), pad8(v), pad8(lw), pad8(a), pad8(g), *scan_par,
                                state_wkv, wkv_s, Bs, TS8, TS8, G=8)
        ya = ya8.reshape(Bs, TS8, D)[:, :Ts].reshape(Ms, D)
        p_new = P[:, o3:o3 + D].reshape(Bs, Ts, D)
        ext = jnp.concatenate([state_pool[l], p_new], axis=1)
        extp = jnp.pad(ext, ((0, 0), (PG - POOL_BUF - Ts, 0), (0, 0))).reshape(Bs * PG, D)
        yb = _pool_ext(extp, l, D, pw16, psr, group_rows=PG, out_row0=PG - Ts, pos0=PAST_LEN)
        yb = yb.reshape(Bs, PG, D)[:, PG - Ts:].reshape(Ms, D)
        y1 = _merge_out(P, l, D, ya, yb, wout16)
        xs, z2 = _resnorm(xs, gain(n2r, l), res=(y1, mod_s, l, 2, 1), mod=(mod_s, l, 4, 3, 1), tm=tms,
                          z_t=True)
        ys = peer(z2)
        outs["shift_s"].append(shift_out(P, Bs, Ts))
        outs["pool_s"].append(ext[:, -POOL_BUF:])

    fin = (final_g.reshape(1, D), pl.BlockSpec((1, D), lambda i: (0, 0)))
    _, y_prompt = _resnorm(xp, fin, res=(yp, mod_p, L - 1, 5, bpg), z_dtype=F32, tm=tmn, y_t=True)
    _, y_sample = _resnorm(xs, fin, res=(ys, mod_s, L - 1, 5, 1), z_dtype=F32, tm=tms, y_t=True)
    st = lambda name: jnp.stack(outs[name], axis=0)
    return (y_prompt.reshape(Bp, Tp, D), y_sample.reshape(Bs, Ts, D),
            wkv_p, st("shift_p"), st("pool_p"), wkv_s, st("shift_s"), st("pool_s"))
```

```python
import functools

import jax
import jax.numpy as jnp
from jax import lax
from jax.experimental import pallas as pl
from jax.experimental.pallas import tpu as pltpu

F32 = jnp.float32
BF16 = jnp.bfloat16

HEAD_DIM = 64
LANES = 128
MXU_WIDTH = 256
GN_EPS = 64e-5
NORM_EPS = 1e-6
PAST_LEN = 16384
POOL_WINDOWS = (2, 4, 8, 16)
POOL_BUF = 15
PEER_TOPK = 16
N_KEYS = 128
SCAN_ROWS = 64
VMEM_LIMIT = 56 * 1024 * 1024


def _cp(*sem):
    return pltpu.CompilerParams(dimension_semantics=sem, vmem_limit_bytes=VMEM_LIMIT)


def _rup(n, m):
    return (n + m - 1) // m * m


def _dot(a, b):
    return jnp.dot(a, b, preferred_element_type=F32)


def _dot_nt(a, b):
    return lax.dot_general(a, b, (((1,), (1,)), ((), ())), preferred_element_type=F32)


def _lspec(l, block, index_map):
    return pl.BlockSpec((None,) + tuple(block), lambda *g: (l,) + tuple(index_map(*g)))


def _mod_kernel(c_ref, w_ref, b_ref, o_ref):
    c = c_ref[...]
    a = (c * jax.nn.sigmoid(c)).astype(BF16)
    o_ref[...] = _dot(a, w_ref[...].astype(BF16)) + b_ref[...]


def _modulation(c_all, w_mod, b_mod):
    L, D, N = w_mod.shape
    Mc = c_all.shape[0]
    tn = 512
    return pl.pallas_call(
        _mod_kernel,
        grid=(L, N // tn),
        in_specs=[
            pl.BlockSpec((Mc, D), lambda l, j: (0, 0)),
            pl.BlockSpec((None, D, tn), lambda l, j: (l, 0, j)),
            pl.BlockSpec((None, 1, tn), lambda l, j: (l, 0, j)),
        ],
        out_specs=pl.BlockSpec((None, Mc, tn), lambda l, j: (l, 0, j)),
        out_shape=jax.ShapeDtypeStruct((L, Mc, N), F32),
        compiler_params=_cp("parallel", "parallel"),
        name="modulation",
    )(c_all, w_mod, b_mod.reshape(L, 1, N))


def _resnorm_kernel(*refs, has_res, has_mod, y_t, z_t):
    it = iter(refs)
    x_ref = next(it)
    if has_res:
        y_ref = next(it)
        gt_ref = next(it)
    g_ref = next(it)
    if has_mod:
        sc_ref = next(it)
        sh_ref = next(it)
    if has_res:
        xo_ref = next(it)
    z_ref = next(it)
    x = x_ref[...]
    if has_res:
        y = y_ref[...].T if y_t else y_ref[...]
        x = x + gt_ref[...] * y
        xo_ref[...] = x
    ms = jnp.mean(x * x, axis=-1, keepdims=True)
    n = x * lax.rsqrt(ms + NORM_EPS) * g_ref[...]
    if has_mod:
        n = n * (1.0 + sc_ref[...]) + sh_ref[...]
    z_ref[...] = (n.T if z_t else n).astype(z_ref.dtype)


def _resnorm(x, g_spec, *, res=None, mod=None, z_dtype=BF16, tm=256, y_t=False, z_t=False):
    M, D = x.shape
    tm = min(tm, M)
    row = pl.BlockSpec((tm, D), lambda i: (i, 0))
    col = pl.BlockSpec((D, tm), lambda i: (0, i))

    def mspec(m4, l, piece, bpg):
        return pl.BlockSpec((None, None, m4.shape[2], D), lambda i: (l, i // bpg, 0, piece))

    args, specs = [x], [row]
    if res is not None:
        y, m4, l, piece, bpg = res
        args += [y, m4]
        specs += [col if y_t else row, mspec(m4, l, piece, bpg)]
    args.append(g_spec[0])
    specs.append(g_spec[1])
    if mod is not None:
        m4, l, scp, shp, bpg = mod
        args += [m4, m4]
        specs += [mspec(m4, l, scp, bpg), mspec(m4, l, shp, bpg)]
    out_shape, out_specs = [], []
    if res is not None:
        out_shape.append(jax.ShapeDtypeStruct((M, D), F32))
        out_specs.append(row)
    out_shape.append(jax.ShapeDtypeStruct((D, M) if z_t else (M, D), z_dtype))
    out_specs.append(col if z_t else row)
    outs = pl.pallas_call(
        functools.partial(_resnorm_kernel, has_res=res is not None, has_mod=mod is not None, y_t=y_t, z_t=z_t),
        grid=(M // tm,),
        in_specs=specs,
        out_specs=out_specs,
        out_shape=out_shape,
        compiler_params=_cp("parallel"),
        name="resnorm",
    )(*args)
    return outs if res is not None else (None, outs[0])


def _mm_kernel(a_ref, b_ref, o_ref):
    o_ref[...] = _dot(a_ref[...], b_ref[...]).astype(o_ref.dtype)


def _mm(a, b, *, la=None, lb=None, tm=1024, tn=512, out_dtype=F32):
    M, K = a.shape[-2:]
    N = b.shape[-1]
    tm, tn = min(tm, M), min(tn, N)
    a_spec = (pl.BlockSpec((tm, K), lambda i, j: (i, 0)) if la is None
              else _lspec(la, (tm, K), lambda i, j: (i, 0)))
    b_spec = (pl.BlockSpec((K, tn), lambda i, j: (0, j)) if lb is None
              else _lspec(lb, (K, tn), lambda i, j: (0, j)))
    return pl.pallas_call(
        _mm_kernel,
        grid=(M // tm, N // tn),
        in_specs=[a_spec, b_spec],
        out_specs=pl.BlockSpec((tm, tn), lambda i, j: (i, j)),
        out_shape=jax.ShapeDtypeStruct((M, N), out_dtype),
        compiler_params=_cp("parallel", "parallel"),
        name="mm",
    )(a, b)


def _softplus(u):
    return jnp.maximum(u, 0.0) + jnp.log1p(jnp.exp(-jnp.abs(u)))


def _prep_kernel(*refs, seq_len, per_row_state, lw, la):
    (r_ref, k_ref, v_ref, l_ref, hr_ref, hk_ref, hv_ref, hl_ref) = refs[:8]
    n = 8
    if per_row_state:
        sr_ref, sk_ref, sv_ref, sl_ref = refs[n:n + 4]
        n += 4
    (mur_ref, muk_ref, muv_ref, mul_ref, w0_ref, a0_ref, wup_ref, aup_ref, gup_ref) = refs[n:n + 9]
    ro_ref, ko_ref, vo_ref, lwo_ref, ao_ref, go_ref = refs[n + 9:]
    tm = r_ref.shape[0]
    base = pl.program_id(0) * tm
    row = lax.broadcasted_iota(jnp.int32, (tm, 1), 0)
    pos = (base + row) % seq_len
    first = row == 0
    start = pos == 0

    def mixed(x_ref, h_ref, s_ref, mu_ref):
        x = x_ref[...]
        prev = jnp.where(first, h_ref[7:8, :], pltpu.roll(x, 1, 0))
        if per_row_state:
            prev = jnp.where(start, s_ref[...], prev)
        else:
            prev = jnp.where(start, 0.0, prev)
        return x + mu_ref[...] * (prev - x)

    sr = sk = sv = sl = None
    if per_row_state:
        sr, sk, sv, sl = sr_ref, sk_ref, sv_ref, sl_ref
    r = mixed(r_ref, hr_ref, sr, mur_ref)
    k = mixed(k_ref, hk_ref, sk, muk_ref)
    v = mixed(v_ref, hv_ref, sv, muv_ref)
    lo = mixed(l_ref, hl_ref, sl, mul_ref)
    wd = lo[:, :lw]
    ad = lo[:, lw:lw + la]
    gd = lo[:, lw + la:]
    wlin = w0_ref[...] + _dot(jnp.tanh(wd).astype(BF16), wup_ref[...])
    w_log = -_softplus(-wlin) - 0.5
    ro_ref[...] = r
    ko_ref[...] = k
    vo_ref[...] = v
    lwo_ref[...] = -jnp.exp(w_log)
    ao_ref[...] = jax.nn.sigmoid(a0_ref[...] + _dot(ad.astype(BF16), aup_ref[...]))
    go_ref[...] = _dot(jax.nn.sigmoid(gd).astype(BF16), gup_ref[...])


def _rwkv_prep(P, l, D, LB, lw, la, seq_len, state_rows, mu_rkv, mu_l, w0, a0, wup, aup, gup, tm=128):
    M = P.shape[0]
    tm = min(tm, M)
    lcol = 6 * D // LB
    per_row_state = state_rows is not None

    def cur(w, c):
        return pl.BlockSpec((tm, w), lambda i: (i, c))

    def halo(w, c):
        return pl.BlockSpec((8, w), lambda i: (jnp.maximum(i * (tm // 8) - 1, 0), c))

    args = [P] * 8
    specs = [cur(D, 0), cur(D, 1), cur(D, 2), cur(LB, lcol),
             halo(D, 0), halo(D, 1), halo(D, 2), halo(LB, lcol)]
    if per_row_state:
        s_rkv, s_l = state_rows
        args += [s_rkv, s_rkv, s_rkv, s_l]
        specs += [_lspec(l, (tm, D), lambda i, c=c: (i, c)) for c in range(3)]
        specs += [_lspec(l, (tm, LB), lambda i: (i, 0))]

    def const(arr, c=0, w=None):
        w = arr.shape[-1] if w is None else w
        return _lspec(l, (arr.shape[-2], w), lambda i: (0, c))

    args += [mu_rkv, mu_rkv, mu_rkv, mu_l, w0, a0, wup, aup, gup]
    specs += [const(mu_rkv, 0, D), const(mu_rkv, 1, D), const(mu_rkv, 2, D), const(mu_l),
              const(w0), const(a0), const(wup), const(aup), const(gup)]
    out = jax.ShapeDtypeStruct((M, D), F32)
    return pl.pallas_call(
        functools.partial(_prep_kernel, seq_len=seq_len, per_row_state=per_row_state, lw=lw, la=la),
        grid=(M // tm,),
        in_specs=specs,
        out_specs=[pl.BlockSpec((tm, D), lambda i: (i, 0))] * 6,
        out_shape=[out] * 6,
        compiler_params=_cp("parallel"),
        name="rwkv_prep",
    )(*args)


def _split3(x):
    h = x.astype(BF16)
    r1 = x - h.astype(F32)
    m = r1.astype(BF16)
    lo = (r1 - m.astype(F32)).astype(BF16)
    return h, m, lo


def _scan_kernel(*refs, G, Tc, zero_init):
    (r_ref, k_ref, v_ref, lw_ref, a_ref, g_ref, kk_ref, ka_ref, rk_ref, lnw_ref, lnb_ref) = refs[:11]
    s0_ref = None if zero_init else refs[11]
    y_ref, so_ref, st_ref = refs[-3:]
    C = SCAN_ROWS
    nseq = C // Tc
    Q = range(G)

    @pl.when(pl.program_id(2) == 0)
    def _():
        if zero_init:
            st_ref[...] = jnp.zeros_like(st_ref)
        else:
            z = jnp.zeros((HEAD_DIM, HEAD_DIM), F32)
            for s in range(nseq):
                for q in Q:
                    top = jnp.concatenate([s0_ref[s, 2 * q], z], axis=1)
                    bot = jnp.concatenate([z, s0_ref[s, 2 * q + 1]], axis=1)
                    st_ref[s, q] = jnp.concatenate([top, bot], axis=0)

    lane = lax.broadcasted_iota(jnp.int32, (1, LANES), 1)
    m_a = lane < HEAD_DIM
    fa = m_a.astype(F32)
    fb = 1.0 - fa
    ri = lax.broadcasted_iota(jnp.int32, (C, LANES), 0)
    cj = lax.broadcasted_iota(jnp.int32, (C, LANES), 1)
    sj = jnp.where(cj >= C, cj - C, cj)
    far = 1 << 20
    sj = jnp.where((ri // Tc) == (sj // Tc), sj, far)
    strict = sj < ri
    incl = sj <= ri
    col_a = (cj < C).astype(F32)
    col_b = 1.0 - col_a
    ti = lax.broadcasted_iota(jnp.int32, (C, 3 * C), 0)
    tj = lax.broadcasted_iota(jnp.int32, (C, 3 * C), 1) % C
    tri3 = jnp.where(jnp.where((ti // Tc) == (tj // Tc), tj, far) <= ti, 1.0, 0.0).astype(BF16)
    bi = lax.broadcasted_iota(jnp.int32, (LANES, LANES), 0)
    bj = lax.broadcasted_iota(jnp.int32, (LANES, LANES), 1)
    bd = (bi < HEAD_DIM) == (bj < HEAD_DIM)
    eye = (bi == bj).astype(F32)
    eye16 = eye.astype(BF16)
    levels = Tc.bit_length() - 1

    def head_sum(x):
        sa = jnp.sum(x * fa, axis=1, keepdims=True)
        sb = jnp.sum(x * fb, axis=1, keepdims=True)
        return jnp.where(m_a, sa, sb)

    def stack(x):
        return jnp.concatenate([x * fa, x * fb], axis=0)

    def seq_rows(x, s):
        return x[s * Tc:(s + 1) * Tc]

    sls = [slice(q * LANES, (q + 1) * LANES) for q in Q]
    r = [r_ref[:, sl] for sl in sls]
    k = [k_ref[:, sl] for sl in sls]
    v = [v_ref[:, sl] for sl in sls]
    lw = [lw_ref[:, sl] for sl in sls]
    a = [a_ref[:, sl] for sl in sls]
    states = [[st_ref[s, q] for s in range(nseq)] for q in Q]
    cum = [_dot(tri3, jnp.concatenate(_split3(lw[q]), axis=0)) for q in Q]
    kk = [k[q] * kk_ref[:, sls[q]] for q in Q]
    kk = [kk[q] / jnp.maximum(jnp.sqrt(head_sum(kk[q] * kk[q])), 1e-12) for q in Q]
    km = [k[q] * (1.0 + (a[q] - 1.0) * ka_ref[:, sls[q]]) for q in Q]
    wcum = [jnp.exp(cum[q]) for q in Q]
    winv = [jnp.exp(-cum[q]) for q in Q]
    rt = [r[q] * wcum[q] for q in Q]
    at = [-kk[q] * jnp.exp(cum[q] - lw[q]) for q in Q]
    bt = [kk[q] * a[q] * winv[q] for q in Q]
    kt = [km[q] * winv[q] for q in Q]
    x16 = [jnp.concatenate([at[q], rt[q]], axis=0).astype(BF16) for q in Q]
    vst = [stack(v[q]).astype(BF16) for q in Q]
    amat = [_dot_nt(x16[q], jnp.concatenate([stack(bt[q]), stack(kt[q])], axis=0).astype(BF16)) for q in Q]
    if nseq == 1:
        xs = [_dot_nt(x16[q], states[q][0].astype(BF16)) for q in Q]
        xs_a = [xs[q][:C] for q in Q]
        xs_r = [xs[q][C:] for q in Q]
    else:
        parts = [[_dot_nt(jnp.concatenate([seq_rows(at[q], s), seq_rows(rt[q], s)], axis=0).astype(BF16),
                          states[q][s].astype(BF16)) for s in range(nseq)] for q in Q]
        xs_a = [jnp.concatenate([p_[:Tc] for p_ in parts[q]], axis=0) for q in Q]
        xs_r = [jnp.concatenate([p_[Tc:] for p_ in parts[q]], axis=0) for q in Q]
    a_ab = [jnp.where(strict, amat[q][:C, :LANES], 0.0) for q in Q]
    a_ak = [jnp.where(strict, amat[q][:C, LANES:], 0.0) for q in Q]
    a_rb = [jnp.where(incl, amat[q][C:, :LANES], 0.0) for q in Q]
    a_rk = [jnp.where(incl, amat[q][C:, LANES:], 0.0) for q in Q]
    rhs = [xs_a[q] + _dot(a_ak[q].astype(BF16), vst[q]) for q in Q]
    nbd = [jnp.concatenate([a_ab[q] * col_a, a_ab[q] * col_b], axis=0) for q in Q]
    tinv = [eye + nbd[q] for q in Q]
    if levels > 1:
        n16 = [nbd[q].astype(BF16) for q in Q]
        pw = [_dot(n16[q], n16[q]) for q in Q]
        for lev in range(1, levels):
            p16 = [pw[q].astype(BF16) for q in Q]
            if lev + 1 < levels:
                pt = [_dot(p16[q], jnp.concatenate([p16[q], tinv[q].astype(BF16)], axis=1)) for q in Q]
                pw = [pt[q][:, :LANES] for q in Q]
                tinv = [tinv[q] + pt[q][:, LANES:] for q in Q]
            else:
                tinv = [tinv[q] + _dot(p16[q], tinv[q].astype(BF16)) for q in Q]
    ust = [_dot(tinv[q].astype(BF16), stack(rhs[q]).astype(BF16)) for q in Q]
    u = [ust[q][:C] + ust[q][C:] for q in Q]
    y = [xs_r[q] + _dot(jnp.concatenate([a_rb[q], a_rk[q]], axis=1).astype(BF16),
                        jnp.concatenate([ust[q].astype(BF16), vst[q]], axis=0)) for q in Q]
    uvt = [[_dot_nt(eye16, jnp.concatenate([seq_rows(u[q], s), seq_rows(v[q], s)], axis=0).astype(BF16))
            for s in range(nseq)] for q in Q]
    upd = [[_dot(uvt[q][s].astype(BF16),
                 jnp.concatenate([seq_rows(bt[q], s), seq_rows(kt[q], s)], axis=0).astype(BF16))
            for s in range(nseq)] for q in Q]
    for q in Q:
        for s in range(nseq):
            st_ref[s, q] = ((states[q][s] + jnp.where(bd, upd[q][s], 0.0))
                            * wcum[q][(s + 1) * Tc - 1:(s + 1) * Tc, :])
    for q in Q:
        sl = sls[q]
        mean = head_sum(y[q]) * (1.0 / HEAD_DIM)
        d = y[q] - mean
        var = head_sum(d * d) * (1.0 / HEAD_DIM)
        yn = d * lax.rsqrt(var + GN_EPS) * lnw_ref[:, sl] + lnb_ref[:, sl]
        bonus = head_sum(r[q] * km[q] * rk_ref[:, sl]) * v[q]
        y_ref[:, sl] = (yn + bonus) * g_ref[:, sl]

    @pl.when(pl.program_id(2) == pl.num_programs(2) - 1)
    def _():
        for s in range(nseq):
            for q in Q:
                full = st_ref[s, q]
                so_ref[s, 2 * q] = full[:HEAD_DIM, :HEAD_DIM]
                so_ref[s, 2 * q + 1] = full[HEAD_DIM:, HEAD_DIM:]


def _rwkv_scan(l, L, r, k, v, lw, a, g, k_k, k_a, r_k, ln_w, ln_b, s0, states_all, nstates, Tseq, Tc, G=16):
    M, D = r.shape
    C = SCAN_ROWS
    nseq = C // Tc
    nc = Tseq // Tc
    assert 2 * C == LANES and (nc == 1 or nseq == 1)
    G = min(G, D // LANES)
    H = D // HEAD_DIM
    tok = pl.BlockSpec((C, G * LANES), lambda b, p, c: (b * nc + c, p))
    par = _lspec(l, (1, G * LANES), lambda b, p, c: (0, p))
    st_blk = _lspec(l, (nseq, 2 * G, HEAD_DIM, HEAD_DIM), lambda b, p, c: (b, p, 0, 0))
    args = [r, k, v, lw, a, g, k_k, k_a, r_k, ln_w, ln_b]
    specs = [tok] * 6 + [par] * 5
    if s0 is not None:
        args.append(s0)
        specs.append(st_blk)
    aliases = {}
    if states_all is not None:
        aliases[len(args)] = 1
        args.append(states_all)
        specs.append(pl.BlockSpec(memory_space=pl.ANY))
    return pl.pallas_call(
        functools.partial(_scan_kernel, G=G, Tc=Tc, zero_init=s0 is None),
        grid=(M // (C * nc), D // (G * LANES), nc),
        in_specs=specs,
        out_specs=[tok, st_blk],
        out_shape=[jax.ShapeDtypeStruct((M, D), F32),
                   jax.ShapeDtypeStruct((L, nstates, H, HEAD_DIM, HEAD_DIM), F32)],
        scratch_shapes=[pltpu.VMEM((nseq, G, LANES, LANES), F32)],
        input_output_aliases=aliases,
        compiler_params=_cp("parallel", "parallel", "arbitrary"),
        name="rwkv_scan",
    )(*args)


def _pool_kernel(*refs, has_halo, seq_len, group_rows, out_row0, pos0, gd):
    if has_halo:
        p_ref, h_ref, pw_ref, ps_ref, o_ref = refs
    else:
        p_ref, pw_ref, ps_ref, o_ref = refs
    tm = p_ref.shape[0]
    if has_halo:
        base = (pl.program_id(0) * tm) % seq_len
        halo = jnp.where(base == 0, 0.0, h_ref[...])
        ext = jnp.concatenate([halo, p_ref[...]], axis=0)
        off = h_ref.shape[0]
        row = lax.broadcasted_iota(jnp.int32, (tm, 1), 0)
        pos = base + row + pos0
    else:
        ext = p_ref[...]
        off = 0
        row = lax.broadcasted_iota(jnp.int32, (tm, 1), 0)
        pos = pos0 + (row % group_rows) - out_row0
    for gi, w in enumerate(POOL_WINDOWS):
        x = ext[:, gi * gd:(gi + 1) * gd]
        s = x
        span = 1
        while span < w:
            s = s + pltpu.roll(s, span, 0)
            span *= 2
        cnt = jnp.minimum(w, pos + 1).astype(F32)
        mixed = s[off:] / cnt - x[off:]
        y = _dot(mixed.astype(BF16), pw_ref[gi])
        o_ref[:, gi * gd:(gi + 1) * gd] = y * ps_ref[:, gi * gd:(gi + 1) * gd]


def _pool(P, l, pcol, D, pw16, ps, *, seq_len, tm=256):
    M = P.shape[0]
    tm = min(tm, seq_len)
    G = len(POOL_WINDOWS)
    gd = D // G
    return pl.pallas_call(
        functools.partial(_pool_kernel, has_halo=True, seq_len=seq_len, group_rows=1, out_row0=0,
                          pos0=0, gd=gd),
        grid=(M // tm,),
        in_specs=[pl.BlockSpec((tm, D), lambda i: (i, pcol)),
                  pl.BlockSpec((16, D), lambda i: (jnp.maximum(i * (tm // 16) - 1, 0), pcol)),
                  _lspec(l, (G, gd, gd), lambda i: (0, 0, 0)),
                  _lspec(l, (1, D), lambda i: (0, 0))],
        out_specs=pl.BlockSpec((tm, D), lambda i: (i, 0)),
        out_shape=jax.ShapeDtypeStruct((M, D), F32),
        compiler_params=_cp("parallel"),
        name="pool",
    )(P, P, pw16, ps)


def _pool_ext(ext, l, D, pw16, ps, *, group_rows, out_row0, pos0, tm=384):
    M = ext.shape[0]
    tm = min(tm, M)
    G = len(POOL_WINDOWS)
    gd = D // G
    return pl.pallas_call(
        functools.partial(_pool_kernel, has_halo=False, seq_len=1, group_rows=group_rows,
                          out_row0=out_row0, pos0=pos0, gd=gd),
        grid=(M // tm,),
        in_specs=[pl.BlockSpec((tm, D), lambda i: (i, 0)),
                  _lspec(l, (G, gd, gd), lambda i: (0, 0, 0)),
                  _lspec(l, (1, D), lambda i: (0, 0))],
        out_specs=pl.BlockSpec((tm, D), lambda i: (i, 0)),
        out_shape=jax.ShapeDtypeStruct((M, D), F32),
        compiler_params=_cp("parallel"),
        name="pool_ext",
    )(ext, pw16, ps)


def _merge_out_kernel(ga_ref, gb_ref, ya_ref, yb_ref, w_ref, o_ref):
    m = jax.nn.sigmoid(ga_ref[...]) * ya_ref[...] + jax.nn.sigmoid(gb_ref[...]) * yb_ref[...]
    o_ref[...] = _dot(m.astype(BF16), w_ref[...])


def _merge_out(P, l, D, ya, yb, wout16, tm=256):
    M = P.shape[0]
    tm = min(tm, M)
    tok = pl.BlockSpec((tm, D), lambda i: (i, 0))
    return pl.pallas_call(
        _merge_out_kernel,
        grid=(M // tm,),
        in_specs=[pl.BlockSpec((tm, D), lambda i: (i, 4)),
                  pl.BlockSpec((tm, D), lambda i: (i, 5)),
                  tok, tok,
                  _lspec(l, (D, D), lambda i: (0, 0))],
        out_specs=tok,
        out_shape=jax.ShapeDtypeStruct((M, D), F32),
        compiler_params=_cp("parallel"),
        name="merge_out",
    )(P, P, ya, yb, wout16)


PEER_RANKS = PEER_TOPK + 1
PEER_RANK_ROWS = _rup(PEER_RANKS, 8)
PEER_PAIR_COUNTS = tuple(min(PEER_RANKS, PEER_RANKS // (i + 1)) for i in range(PEER_RANKS))
PEER_CAND_ROWS = _rup(sum(PEER_PAIR_COUNTS), 8)


def _peer_select_kernel(qt_ref, keys_ref, ethr_ref, e1_ref, e2_ref,
                        s1_scr, s2_ref, t1_scr, t2_scr, cand_scr, top_scr, *, heads):
    half = N_KEYS
    tl = qt_ref.shape[1]
    neg = -jnp.inf
    for h in range(heads):
        q1 = qt_ref[2 * h * half:(2 * h + 1) * half, :].astype(BF16)
        q2 = qt_ref[(2 * h + 1) * half:(2 * h + 2) * half, :].astype(BF16)
        s1_scr[h] = _dot(keys_ref[0], q1)
        s2_ref[h] = _dot(keys_ref[1], q2)

    def next_max(x, prev):
        return jnp.max(jnp.where(x < prev, x, neg), axis=0, keepdims=True)

    def rank_body(i, prev):
        new = []
        for h in range(heads):
            m1 = next_max(s1_scr[h], prev[2 * h])
            m2 = next_max(s2_ref[h], prev[2 * h + 1])
            t1_scr[h, pl.ds(i, 1), :] = m1
            t2_scr[h, pl.ds(i, 1), :] = m2
            new += [m1, m2]
        return tuple(new)

    inf_row = jnp.full((1, tl), jnp.inf, F32)
    lax.fori_loop(0, PEER_RANKS, rank_body, (inf_row,) * (2 * heads))

    for h in range(heads):
        cand_scr[h, PEER_CAND_ROWS - 8:, :] = jnp.full((8, tl), neg, F32)
        r0 = 0
        for i, n in enumerate(PEER_PAIR_COUNTS):
            cand_scr[h, r0:r0 + n, :] = t1_scr[h, i:i + 1, :] + t2_scr[h, 0:n, :]
            r0 += n

    def cand_body(i, prev):
        new = []
        for h in range(heads):
            m = next_max(cand_scr[h], prev[h])
            top_scr[h, pl.ds(i, 1), :] = m
            new.append(m)
        return tuple(new)

    lax.fori_loop(0, PEER_RANKS, cand_body, (inf_row,) * heads)

    for h in range(heads):
        top = top_scr[h]
        t16 = top[PEER_TOPK - 1:PEER_TOPK]
        t17 = top[PEER_TOPK:PEER_TOPK + 1]
        mid = jnp.where(t17 > neg, 0.5 * (t16 + t17), t16)
        z = jnp.sum(jnp.exp(top[:PEER_TOPK] - top[0:1]), axis=0, keepdims=True)
        s1 = s1_scr[h]
        m2 = t2_scr[h, 0:1, :]
        ethr_ref[h] = jnp.exp((mid - m2) - s1)
        e1_ref[h] = jnp.exp(s1 - t1_scr[h, 0:1, :]) / z
        e2_ref[h] = jnp.exp(s2_ref[h] - m2)


def _peer_select(qt, keys16, l, heads, tl=128):
    QD, M = qt.shape
    tl = min(tl, M)
    big = jax.ShapeDtypeStruct((heads, N_KEYS, M), F32)
    bspec = pl.BlockSpec((heads, N_KEYS, tl), lambda i: (0, 0, i))
    ranks = pltpu.VMEM((heads, PEER_RANK_ROWS, tl), F32)
    return pl.pallas_call(
        functools.partial(_peer_select_kernel, heads=heads),
        grid=(M // tl,),
        in_specs=[pl.BlockSpec((QD, tl), lambda i: (0, i)),
                  _lspec(l, keys16.shape[1:], lambda i: (0, 0, 0))],
        out_specs=[bspec] * 3,
        out_shape=[big] * 3,
        scratch_shapes=[pltpu.VMEM((heads, N_KEYS, tl), F32), pltpu.VMEM((heads, N_KEYS, tl), F32), ranks, ranks,
                        pltpu.VMEM((heads, PEER_CAND_ROWS, tl), F32), ranks],
        compiler_params=_cp("parallel"),
        name="peer_select",
    )(qt, keys16)


PEER_LANE_TILE = 128
PEER_STEP_KEYS = 4
PEER_UNIT_KEYS = 1
PEER_UNIT_GROUPS = 16


def _gelu_tanh(x):
    return 0.5 * x * (1.0 + jnp.tanh(0.7978845608028654 * (x + 0.044715 * (x * x * x))))


def _peer_dense_kernel(zt_ref, u_ref, vt_ref, thr_ref, e1_ref, e2_ref, o_ref, ht0, ht1, wt0, wt1, *, heads, nblk):
    s = pl.program_id(1)
    tm = zt_ref.shape[1]
    lt = min(PEER_LANE_TILE, tm)
    ka = PEER_STEP_KEYS
    uk = PEER_UNIT_KEYS
    groups = N_KEYS // 8
    mxu_n = min(MXU_WIDTH, tm)
    n_lane = tm // mxu_n
    d_out = o_ref.shape[0]
    ug = PEER_UNIT_GROUPS
    units = [(t0, ap, g0) for t0 in range(0, tm, lt) for ap in range(0, ka, uk) for g0 in range(0, groups, ug)]
    nu = len(units)

    @pl.when(s == 0)
    def _():
        o_ref[...] = jnp.zeros_like(o_ref)
        for buf in (ht0, ht1, wt0, wt1):
            buf[...] = jnp.zeros_like(buf)

    valid = jnp.logical_and(s >= 1, s <= nblk)

    def order_token(res):
        bits = pltpu.bitcast(res[:8, :lt], jnp.int32)
        bits = lax.shift_right_logical(lax.shift_right_logical(bits, 16), 16)
        return bits.astype(F32)

    def body(ht_new, ht_old, wt_new, wt_old, k0):
        n_sc = 2 * n_lane
        sc_rows = ka * N_KEYS // 2
        n_cb = 4 * n_lane
        ob = d_out // (n_cb // n_lane)

        def scores(piece):
            rs = slice((piece // n_lane) * sc_rows, (piece // n_lane + 1) * sc_rows)
            ts = slice((piece % n_lane) * mxu_n, (piece % n_lane + 1) * mxu_n)
            res = _dot(u_ref[rs, :], zt_ref[:, ts])
            ht_new[rs, ts] = res
            return order_token(res)

        def combine(piece):
            os_ = slice((piece // n_lane) * ob, (piece // n_lane + 1) * ob)
            ts = slice((piece % n_lane) * mxu_n, (piece % n_lane + 1) * mxu_n)
            res = _dot(vt_ref[os_, :], wt_new[:, ts])
            o_ref[os_, ts] += res
            return order_token(res)

        def weights(unit, start):
            t0, ap, g0 = units[unit]
            ts = slice(t0, t0 + lt)
            accs = [[start for _ in range(ug)] for _ in range(uk)]
            for h in range(heads):
                thr_b = [jnp.broadcast_to(thr_ref[h, k0 + ap + d:k0 + ap + d + 1, ts], (8, lt)) for d in range(uk)]
                e1_b = [jnp.broadcast_to(e1_ref[h, k0 + ap + d:k0 + ap + d + 1, ts], (8, lt)) for d in range(uk)]
                for g in range(ug):
                    e2t = e2_ref[h, (g0 + g) * 8:(g0 + g + 1) * 8, ts]
                    for d in range(uk):
                        accs[d][g] = accs[d][g] + jnp.where(e2t >= thr_b[d], e1_b[d] * e2t, 0.0)
            for d in range(uk):
                for g in range(0, ug, 2):
                    rs = slice((ap + d) * N_KEYS + (g0 + g) * 8, (ap + d) * N_KEYS + (g0 + g + 2) * 8)
                    w = jnp.concatenate([accs[d][g], accs[d][g + 1]], axis=0) * _gelu_tanh(ht_old[rs, ts])
                    wt_old[rs, ts] = jnp.where(valid, w, 0.0).astype(BF16)

        zero = jnp.zeros((8, lt), F32)
        pending, ready = [], []
        sc_every, cb_every = max(nu // n_sc, 1), max(nu // n_cb, 1)
        for unit in range(nu):
            if unit % cb_every == 0 and unit // cb_every < n_cb:
                pending.append(combine(unit // cb_every))
            if unit % sc_every == 0 and unit // sc_every < n_sc:
                pending.append(scores(unit // sc_every))
            start = zero
            for tok in ready:
                start = start + tok
            ready, pending = pending, []
            weights(unit, start)

    @pl.when(s % 2 == 0)
    def _():
        body(ht0, ht1, wt0, wt1, (8 - ka) % 8)

    @pl.when(s % 2 == 1)
    def _():
        body(ht1, ht0, wt1, wt0, 0)


def _peer_dense(zt, u16, vt16, l, ethr, e1, e2, tm=1024):
    D, M = zt.shape
    E = u16.shape[1]
    heads = e2.shape[0]
    tm = min(tm, M)
    ka = PEER_STEP_KEYS
    ec = ka * N_KEYS
    nblk = E // ec
    blk = lambda s_, back: jnp.clip(s_ - back, 0, nblk - 1)
    assert ka in (4, 8)
    rows_prev = pl.BlockSpec((heads, 8, tm), lambda i, s_: (0, blk(s_, 1) * ka // 8, i))
    buf_f32 = pltpu.VMEM((ec, tm), F32)
    buf_b16 = pltpu.VMEM((ec, tm), BF16)
    return pl.pallas_call(
        functools.partial(_peer_dense_kernel, heads=heads, nblk=nblk),
        grid=(M // tm, nblk + 2),
        in_specs=[pl.BlockSpec((D, tm), lambda i, s_: (0, i)),
                  _lspec(l, (ec, D), lambda i, s_: (blk(s_, 0), 0)),
                  _lspec(l, (D, ec), lambda i, s_: (0, blk(s_, 2))),
                  rows_prev, rows_prev,
                  pl.BlockSpec((heads, N_KEYS, tm), lambda i, s_: (0, 0, i))],
        out_specs=pl.BlockSpec((D, tm), lambda i, s_: (0, i)),
        out_shape=jax.ShapeDtypeStruct((D, M), F32),
        scratch_shapes=[buf_f32, buf_f32, buf_b16, buf_b16],
        compiler_params=_cp("parallel", "arbitrary"),
        name="peer_dense",
    )(zt, u16, vt16, ethr, e1, e2)


def _pad_last(w, n):
    return jnp.pad(w, [(0, 0)] * (w.ndim - 1) + [(0, n - w.shape[-1])])


def _pad_rows(w, n):
    return jnp.pad(w, [(0, 0)] * (w.ndim - 2) + [(0, n - w.shape[-2]), (0, 0)])


def kernel(x_prompt, x_sample, c_prompt, c_sample, state_wkv, state_shift, state_pool, norm1_g, norm2_g, final_g, w_mod, b_mod, w_in, mu_shift, w0, w_up, a0, a_up, g_up, k_k, k_a, r_k, ln_w, ln_b, pool_w, pool_scale, w_out, w_q, sub_keys, peer_u, peer_v):
    Bp, Tp, D = x_prompt.shape
    Bs, Ts, _ = x_sample.shape
    L = w_in.shape[0]
    H = D // HEAD_DIM
    dw, da, dg = w_up.shape[1], a_up.shape[1], g_up.shape[1]
    LW, LA, LG = _rup(dw, LANES), _rup(da, LANES), _rup(dg, LANES)
    LB = LW + LA + LG
    o3 = 3 * D
    o4, o5, o6 = o3 + dw, o3 + dw + da, o3 + dw + da + dg
    peer_heads = w_q.shape[2] // (2 * N_KEYS)
    Mp, Ms = Bp * Tp, Bs * Ts
    TS8 = 8
    PG = 24
    assert Tp % SCAN_ROWS == 0 and Ts <= TS8 and POOL_BUF + Ts <= PG

    Mc = _rup(Bp + Bs, 8)
    c_all = jnp.pad(jnp.concatenate([c_prompt, c_sample], axis=0), ((0, Mc - Bp - Bs), (0, 0)))
    mod_all = _modulation(c_all, w_mod, b_mod)
    tmn, tms = min(256, Tp), min(256, Ms)
    bpg = Tp // tmn
    mod_p = mod_all[:, :Bp].reshape(L, Bp, 1, 6 * D)
    mod_s = jnp.repeat(mod_all[:, Bp:Bp + Bs], Ts, axis=1).reshape(L, Ms // tms, tms, 6 * D)

    def repack_cols(t):
        return t[..., :o3], jnp.concatenate(
            [_pad_last(t[..., o3:o4], LW), _pad_last(t[..., o4:o5], LA), _pad_last(t[..., o5:o6], LG)], axis=-1)

    w_rkv, w_l = repack_cols(w_in)
    winp = jnp.concatenate([w_rkv, w_in[..., o6:], w_l], axis=-1).astype(BF16)
    mu_rkv, mu_l = repack_cols(mu_shift[:, None, :])
    wup = _pad_rows(w_up, LW).astype(BF16)
    aup = _pad_rows(a_up, LA).astype(BF16)
    gup = _pad_rows(g_up, LG).astype(BF16)
    rows = lambda t: t.reshape(L, 1, D)
    w0r, a0r, kkr, kar, rkr, lnwr, lnbr, psr = (rows(t) for t in (w0, a0, k_k, k_a, r_k, ln_w, ln_b, pool_scale))
    n1r, n2r = rows(norm1_g), rows(norm2_g)
    pw16 = pool_w.astype(BF16)
    wout16 = w_out.astype(BF16)
    wqt16 = jnp.swapaxes(w_q, 1, 2).astype(BF16)
    keys16 = sub_keys.astype(BF16)
    u16 = peer_u.astype(BF16)
    vt16 = jnp.swapaxes(peer_v, 1, 2).astype(BF16)
    gain = lambda arr, l: (arr, _lspec(l, (1, D), lambda i: (0, 0)))
    st_rkv, st_l = repack_cols(state_shift)
    expand = lambda t: jnp.pad(t[:, :, None, :], ((0, 0), (0, 0), (0, Ts - 1), (0, 0))).reshape(L, Ms, -1)
    st_rkv, st_l = expand(st_rkv), expand(st_l)

    def shift_out(P, nseq, T):
        last = P.reshape(nseq, T, -1)[:, -1]
        return jnp.concatenate(
            [last[:, :o3], last[:, 6 * D:6 * D + dw], last[:, 6 * D + LW:6 * D + LW + da],
             last[:, 6 * D + LW + LA:6 * D + LW + LA + dg]], axis=-1)

    xp = x_prompt.reshape(Mp, D)
    xs = x_sample.reshape(Ms, D)
    yp = ys = None
    outs = {k_: [] for k_ in ("shift_p", "pool_p", "shift_s", "pool_s")}
    wkv_p = wkv_s = None

    for l in range(L):
        def half_layer_1(x, y_prev, mod4, bpg_, tm_):
            res = None if y_prev is None else (y_prev, mod4, l - 1, 5, bpg_)
            x_new, z = _resnorm(x, gain(n1r, l), res=res, mod=(mod4, l, 1, 0, bpg_), tm=tm_, y_t=True)
            return (x if res is None else x_new), _mm(z, winp, lb=l)

        def peer(zt):
            qt = _mm(wqt16, zt, la=l, tm=wqt16.shape[1], tn=512)
            ethr, e1, e2 = _peer_select(qt, keys16, l, peer_heads)
            return _peer_dense(zt, u16, vt16, l, ethr, e1, e2)

        scan_par = (kkr, kar, rkr, lnwr, lnbr)

        xp, P = half_layer_1(xp, yp, mod_p, bpg, tmn)
        r, k, v, lw, a, g = _rwkv_prep(P, l, D, LB, LW, LA, Tp, None, mu_rkv, mu_l, w0r, a0r, wup, aup, gup)
        ya, wkv_p = _rwkv_scan(l, L, r, k, v, lw, a, g, *scan_par, None, wkv_p, Bp, Tp, SCAN_ROWS)
        yb = _pool(P, l, 3, D, pw16, psr, seq_len=Tp)
        y1 = _merge_out(P, l, D, ya, yb, wout16)
        xp, z2 = _resnorm(xp, gain(n2r, l), res=(y1, mod_p, l, 2, bpg), mod=(mod_p, l, 4, 3, bpg), tm=tmn,
                          z_t=True)
        yp = peer(z2)
        outs["shift_p"].append(shift_out(P, Bp, Tp))
        outs["pool_p"].append(P.reshape(Bp, Tp, -1)[:, Tp - POOL_BUF:, o3:o3 + D])

        xs, P = half_layer_1(xs, ys, mod_s, 1, tms)
        r, k, v, lw, a, g = _rwkv_prep(P, l, D, LB, LW, LA, Ts, (st_rkv, st_l), mu_rkv, mu_l,
                                       w0r, a0r, wup, aup, gup)
        pad8 = lambda t: jnp.pad(t.reshape(Bs, Ts, D), ((0, 0), (0, TS8 - Ts), (0, 0))).reshape(Bs * TS8, D)
        ya8, wkv_s = _rwkv_scan(l, L, pad8(r), pad8(k), pad8(v), pad8(lw), pad8(a), pad8(g), *scan_par,
                                state_wkv, wkv_s, Bs, TS8, TS8, G=8)
        ya = ya8.reshape(Bs, TS8, D)[:, :Ts].reshape(Ms, D)
        p_new = P[:, o3:o3 + D].reshape(Bs, Ts, D)
        ext = jnp.concatenate([state_pool[l], p_new], axis=1)
        extp = jnp.pad(ext, ((0, 0), (PG - POOL_BUF - Ts, 0), (0, 0))).reshape(Bs * PG, D)
        yb = _pool_ext(extp, l, D, pw16, psr, group_rows=PG, out_row0=PG - Ts, pos0=PAST_LEN)
        yb = yb.reshape(Bs, PG, D)[:, PG - Ts:].reshape(Ms, D)
        y1 = _merge_out(P, l, D, ya, yb, wout16)
        xs, z2 = _resnorm(xs, gain(n2r, l), res=(y1, mod_s, l, 2, 1), mod=(mod_s, l, 4, 3, 1), tm=tms,
                          z_t=True)
        ys = peer(z2)
        outs["shift_s"].append(shift_out(P, Bs, Ts))
        outs["pool_s"].append(ext[:, -POOL_BUF:])

    fin = (final_g.reshape(1, D), pl.BlockSpec((1, D), lambda i: (0, 0)))
    _, y_prompt = _resnorm(xp, fin, res=(yp, mod_p, L - 1, 5, bpg), z_dtype=F32, tm=tmn, y_t=True)
    _, y_sample = _resnorm(xs, fin, res=(ys, mod_s, L - 1, 5, 1), z_dtype=F32, tm=tms, y_t=True)
    st = lambda name: jnp.stack(outs[name], axis=0)
    return (y_prompt.reshape(Bp, Tp, D), y_sample.reshape(Bs, Ts, D),
            wkv_p, st("shift_p"), st("pool_p"), wkv_s, st("shift_s"), st("pool_s"))
```

```python
import functools

import jax
import jax.numpy as jnp
from jax import lax
from jax.experimental import pallas as pl
from jax.experimental.pallas import tpu as pltpu

F32 = jnp.float32
BF16 = jnp.bfloat16

HEAD_DIM = 64
LANES = 128
MXU_WIDTH = 256
GN_EPS = 64e-5
NORM_EPS = 1e-6
PAST_LEN = 16384
POOL_WINDOWS = (2, 4, 8, 16)
POOL_BUF = 15
PEER_TOPK = 16
N_KEYS = 128
SCAN_ROWS = 64
VMEM_LIMIT = 56 * 1024 * 1024


def _cp(*sem):
    return pltpu.CompilerParams(dimension_semantics=sem, vmem_limit_bytes=VMEM_LIMIT)


def _rup(n, m):
    return (n + m - 1) // m * m


def _lane_tile(n, cap):
    return max(t for t in range(LANES, cap + 1, LANES) if n % t == 0)


def _dot(a, b):
    return jnp.dot(a, b, preferred_element_type=F32)


def _dot_nt(a, b):
    return lax.dot_general(a, b, (((1,), (1,)), ((), ())), preferred_element_type=F32)


def _lspec(l, block, index_map):
    return pl.BlockSpec((None,) + tuple(block), lambda *g: (l,) + tuple(index_map(*g)))


def _mod_kernel(c_ref, w_ref, b_ref, o_ref):
    c = c_ref[...]
    a = (c * jax.nn.sigmoid(c)).astype(BF16)
    o_ref[...] = _dot(a, w_ref[...].astype(BF16)) + b_ref[...]


def _modulation(c_all, w_mod, b_mod):
    L, D, N = w_mod.shape
    Mc = c_all.shape[0]
    tn = 512
    return pl.pallas_call(
        _mod_kernel,
        grid=(L, N // tn),
        in_specs=[
            pl.BlockSpec((Mc, D), lambda l, j: (0, 0)),
            pl.BlockSpec((None, D, tn), lambda l, j: (l, 0, j)),
            pl.BlockSpec((None, 1, tn), lambda l, j: (l, 0, j)),
        ],
        out_specs=pl.BlockSpec((None, Mc, tn), lambda l, j: (l, 0, j)),
        out_shape=jax.ShapeDtypeStruct((L, Mc, N), F32),
        compiler_params=_cp("parallel", "parallel"),
        name="modulation",
    )(c_all, w_mod, b_mod.reshape(L, 1, N))


def _resnorm_kernel(*refs, has_res, has_mod, y_t, z_t):
    it = iter(refs)
    x_ref = next(it)
    if has_res:
        y_ref = next(it)
        gt_ref = next(it)
    g_ref = next(it)
    if has_mod:
        sc_ref = next(it)
        sh_ref = next(it)
    if has_res:
        xo_ref = next(it)
    z_ref = next(it)
    x = x_ref[...]
    if has_res:
        y = y_ref[...].T if y_t else y_ref[...]
        x = x + gt_ref[...] * y
        xo_ref[...] = x
    ms = jnp.mean(x * x, axis=-1, keepdims=True)
    n = x * lax.rsqrt(ms + NORM_EPS) * g_ref[...]
    if has_mod:
        n = n * (1.0 + sc_ref[...]) + sh_ref[...]
    z_ref[...] = (n.T if z_t else n).astype(z_ref.dtype)


def _resnorm(x, g_spec, *, res=None, mod=None, z_dtype=BF16, tm=256, y_t=False, z_t=False):
    M, D = x.shape
    tm = min(tm, M)
    row = pl.BlockSpec((tm, D), lambda i: (i, 0))
    col = pl.BlockSpec((D, tm), lambda i: (0, i))

    def mspec(m4, l, piece, bpg):
        return pl.BlockSpec((None, None, m4.shape[2], D), lambda i: (l, i // bpg, 0, piece))

    args, specs = [x], [row]
    if res is not None:
        y, m4, l, piece, bpg = res
        args += [y, m4]
        specs += [col if y_t else row, mspec(m4, l, piece, bpg)]
    args.append(g_spec[0])
    specs.append(g_spec[1])
    if mod is not None:
        m4, l, scp, shp, bpg = mod
        args += [m4, m4]
        specs += [mspec(m4, l, scp, bpg), mspec(m4, l, shp, bpg)]
    out_shape, out_specs = [], []
    if res is not None:
        out_shape.append(jax.ShapeDtypeStruct((M, D), F32))
        out_specs.append(row)
    out_shape.append(jax.ShapeDtypeStruct((D, M) if z_t else (M, D), z_dtype))
    out_specs.append(col if z_t else row)
    outs = pl.pallas_call(
        functools.partial(_resnorm_kernel, has_res=res is not None, has_mod=mod is not None, y_t=y_t, z_t=z_t),
        grid=(M // tm,),
        in_specs=specs,
        out_specs=out_specs,
        out_shape=out_shape,
        compiler_params=_cp("parallel"),
        name="resnorm",
    )(*args)
    return outs if res is not None else (None, outs[0])


def _mm_kernel(a_ref, b_ref, o_ref):
    o_ref[...] = _dot(a_ref[...], b_ref[...]).astype(o_ref.dtype)


def _mm(a, b, *, la=None, lb=None, tm=1024, tn=512, out_dtype=F32):
    M, K = a.shape[-2:]
    N = b.shape[-1]
    tm, tn = min(tm, M), min(tn, N)
    a_spec = (pl.BlockSpec((tm, K), lambda i, j: (i, 0)) if la is None
              else _lspec(la, (tm, K), lambda i, j: (i, 0)))
    b_spec = (pl.BlockSpec((K, tn), lambda i, j: (0, j)) if lb is None
              else _lspec(lb, (K, tn), lambda i, j: (0, j)))
    return pl.pallas_call(
        _mm_kernel,
        grid=(M // tm, N // tn),
        in_specs=[a_spec, b_spec],
        out_specs=pl.BlockSpec((tm, tn), lambda i, j: (i, j)),
        out_shape=jax.ShapeDtypeStruct((M, N), out_dtype),
        compiler_params=_cp("parallel", "parallel"),
        name="mm",
    )(a, b)


def _softplus(u):
    return jnp.maximum(u, 0.0) + jnp.log1p(jnp.exp(-jnp.abs(u)))


def _prep_kernel(*refs, seq_len, per_row_state, lw, la):
    (r_ref, k_ref, v_ref, l_ref, hr_ref, hk_ref, hv_ref, hl_ref) = refs[:8]
    n = 8
    if per_row_state:
        sr_ref, sk_ref, sv_ref, sl_ref = refs[n:n + 4]
        n += 4
    (mur_ref, muk_ref, muv_ref, mul_ref, w0_ref, a0_ref, wup_ref, aup_ref, gup_ref) = refs[n:n + 9]
    ro_ref, ko_ref, vo_ref, lwo_ref, ao_ref, go_ref = refs[n + 9:]
    tm = r_ref.shape[0]
    base = pl.program_id(0) * tm
    row = lax.broadcasted_iota(jnp.int32, (tm, 1), 0)
    pos = (base + row) % seq_len
    first = row == 0
    start = pos == 0

    def mixed(x_ref, h_ref, s_ref, mu_ref):
        x = x_ref[...]
        prev = jnp.where(first, h_ref[7:8, :], pltpu.roll(x, 1, 0))
        if per_row_state:
            prev = jnp.where(start, s_ref[...], prev)
        else:
            prev = jnp.where(start, 0.0, prev)
        return x + mu_ref[...] * (prev - x)

    sr = sk = sv = sl = None
    if per_row_state:
        sr, sk, sv, sl = sr_ref, sk_ref, sv_ref, sl_ref
    r = mixed(r_ref, hr_ref, sr, mur_ref)
    k = mixed(k_ref, hk_ref, sk, muk_ref)
    v = mixed(v_ref, hv_ref, sv, muv_ref)
    lo = mixed(l_ref, hl_ref, sl, mul_ref)
    wd = lo[:, :lw]
    ad = lo[:, lw:lw + la]
    gd = lo[:, lw + la:]
    wlin = w0_ref[...] + _dot(jnp.tanh(wd).astype(BF16), wup_ref[...])
    w_log = -_softplus(-wlin) - 0.5
    ro_ref[...] = r
    ko_ref[...] = k
    vo_ref[...] = v
    lwo_ref[...] = -jnp.exp(w_log)
    ao_ref[...] = jax.nn.sigmoid(a0_ref[...] + _dot(ad.astype(BF16), aup_ref[...]))
    go_ref[...] = _dot(jax.nn.sigmoid(gd).astype(BF16), gup_ref[...])


def _rwkv_prep(P, l, D, LB, lw, la, seq_len, state_rows, mu_rkv, mu_l, w0, a0, wup, aup, gup, tm=128):
    M = P.shape[0]
    tm = min(tm, M)
    lcol = 6 * D // LB
    per_row_state = state_rows is not None

    def cur(w, c):
        return pl.BlockSpec((tm, w), lambda i: (i, c))

    def halo(w, c):
        return pl.BlockSpec((8, w), lambda i: (jnp.maximum(i * (tm // 8) - 1, 0), c))

    args = [P] * 8
    specs = [cur(D, 0), cur(D, 1), cur(D, 2), cur(LB, lcol),
             halo(D, 0), halo(D, 1), halo(D, 2), halo(LB, lcol)]
    if per_row_state:
        s_rkv, s_l = state_rows
        args += [s_rkv, s_rkv, s_rkv, s_l]
        specs += [_lspec(l, (tm, D), lambda i, c=c: (i, c)) for c in range(3)]
        specs += [_lspec(l, (tm, LB), lambda i: (i, 0))]

    def const(arr, c=0, w=None):
        w = arr.shape[-1] if w is None else w
        return _lspec(l, (arr.shape[-2], w), lambda i: (0, c))

    args += [mu_rkv, mu_rkv, mu_rkv, mu_l, w0, a0, wup, aup, gup]
    specs += [const(mu_rkv, 0, D), const(mu_rkv, 1, D), const(mu_rkv, 2, D), const(mu_l),
              const(w0), const(a0), const(wup), const(aup), const(gup)]
    out = jax.ShapeDtypeStruct((M, D), F32)
    return pl.pallas_call(
        functools.partial(_prep_kernel, seq_len=seq_len, per_row_state=per_row_state, lw=lw, la=la),
        grid=(M // tm,),
        in_specs=specs,
        out_specs=[pl.BlockSpec((tm, D), lambda i: (i, 0))] * 6,
        out_shape=[out] * 6,
        compiler_params=_cp("parallel"),
        name="rwkv_prep",
    )(*args)


def _split3(x):
    h = x.astype(BF16)
    r1 = x - h.astype(F32)
    m = r1.astype(BF16)
    lo = (r1 - m.astype(F32)).astype(BF16)
    return h, m, lo


def _scan_kernel(*refs, G, Tc, zero_init):
    (r_ref, k_ref, v_ref, lw_ref, a_ref, g_ref, kk_ref, ka_ref, rk_ref, lnw_ref, lnb_ref) = refs[:11]
    s0_ref = None if zero_init else refs[11]
    y_ref, so_ref, st_ref = refs[-3:]
    C = SCAN_ROWS
    nseq = C // Tc
    Q = range(G)

    @pl.when(pl.program_id(2) == 0)
    def _():
        if zero_init:
            st_ref[...] = jnp.zeros_like(st_ref)
        else:
            z = jnp.zeros((HEAD_DIM, HEAD_DIM), F32)
            for s in range(nseq):
                for q in Q:
                    top = jnp.concatenate([s0_ref[s, 2 * q], z], axis=1)
                    bot = jnp.concatenate([z, s0_ref[s, 2 * q + 1]], axis=1)
                    st_ref[s, q] = jnp.concatenate([top, bot], axis=0)

    lane = lax.broadcasted_iota(jnp.int32, (1, LANES), 1)
    m_a = lane < HEAD_DIM
    fa = m_a.astype(F32)
    fb = 1.0 - fa
    ri = lax.broadcasted_iota(jnp.int32, (C, LANES), 0)
    cj = lax.broadcasted_iota(jnp.int32, (C, LANES), 1)
    sj = jnp.where(cj >= C, cj - C, cj)
    far = 1 << 20
    sj = jnp.where((ri // Tc) == (sj // Tc), sj, far)
    strict = sj < ri
    incl = sj <= ri
    col_a = (cj < C).astype(F32)
    col_b = 1.0 - col_a
    ti = lax.broadcasted_iota(jnp.int32, (C, 3 * C), 0)
    tj = lax.broadcasted_iota(jnp.int32, (C, 3 * C), 1) % C
    tri3 = jnp.where(jnp.where((ti // Tc) == (tj // Tc), tj, far) <= ti, 1.0, 0.0).astype(BF16)
    bi = lax.broadcasted_iota(jnp.int32, (LANES, LANES), 0)
    bj = lax.broadcasted_iota(jnp.int32, (LANES, LANES), 1)
    bd = (bi < HEAD_DIM) == (bj < HEAD_DIM)
    eye = (bi == bj).astype(F32)
    eye16 = eye.astype(BF16)
    levels = Tc.bit_length() - 1

    def head_sum(x):
        sa = jnp.sum(x * fa, axis=1, keepdims=True)
        sb = jnp.sum(x * fb, axis=1, keepdims=True)
        return jnp.where(m_a, sa, sb)

    def stack(x):
        return jnp.concatenate([x * fa, x * fb], axis=0)

    def seq_rows(x, s):
        return x[s * Tc:(s + 1) * Tc]

    sls = [slice(q * LANES, (q + 1) * LANES) for q in Q]
    r = [r_ref[:, sl] for sl in sls]
    k = [k_ref[:, sl] for sl in sls]
    v = [v_ref[:, sl] for sl in sls]
    lw = [lw_ref[:, sl] for sl in sls]
    a = [a_ref[:, sl] for sl in sls]
    states = [[st_ref[s, q] for s in range(nseq)] for q in Q]
    cum = [_dot(tri3, jnp.concatenate(_split3(lw[q]), axis=0)) for q in Q]
    kk = [k[q] * kk_ref[:, sls[q]] for q in Q]
    kk = [kk[q] / jnp.maximum(jnp.sqrt(head_sum(kk[q] * kk[q])), 1e-12) for q in Q]
    km = [k[q] * (1.0 + (a[q] - 1.0) * ka_ref[:, sls[q]]) for q in Q]
    wcum = [jnp.exp(cum[q]) for q in Q]
    winv = [jnp.exp(-cum[q]) for q in Q]
    rt = [r[q] * wcum[q] for q in Q]
    at = [-kk[q] * jnp.exp(cum[q] - lw[q]) for q in Q]
    bt = [kk[q] * a[q] * winv[q] for q in Q]
    kt = [km[q] * winv[q] for q in Q]
    x16 = [jnp.concatenate([at[q], rt[q]], axis=0).astype(BF16) for q in Q]
    vst = [stack(v[q]).astype(BF16) for q in Q]
    amat = [_dot_nt(x16[q], jnp.concatenate([stack(bt[q]), stack(kt[q])], axis=0).astype(BF16)) for q in Q]
    if nseq == 1:
        xs = [_dot_nt(x16[q], states[q][0].astype(BF16)) for q in Q]
        xs_a = [xs[q][:C] for q in Q]
        xs_r = [xs[q][C:] for q in Q]
    else:
        parts = [[_dot_nt(jnp.concatenate([seq_rows(at[q], s), seq_rows(rt[q], s)], axis=0).astype(BF16),
                          states[q][s].astype(BF16)) for s in range(nseq)] for q in Q]
        xs_a = [jnp.concatenate([p_[:Tc] for p_ in parts[q]], axis=0) for q in Q]
        xs_r = [jnp.concatenate([p_[Tc:] for p_ in parts[q]], axis=0) for q in Q]
    a_ab = [jnp.where(strict, amat[q][:C, :LANES], 0.0) for q in Q]
    a_ak = [jnp.where(strict, amat[q][:C, LANES:], 0.0) for q in Q]
    a_rb = [jnp.where(incl, amat[q][C:, :LANES], 0.0) for q in Q]
    a_rk = [jnp.where(incl, amat[q][C:, LANES:], 0.0) for q in Q]
    rhs = [xs_a[q] + _dot(a_ak[q].astype(BF16), vst[q]) for q in Q]
    nbd = [jnp.concatenate([a_ab[q] * col_a, a_ab[q] * col_b], axis=0) for q in Q]
    tinv = [eye + nbd[q] for q in Q]
    if levels > 1:
        n16 = [nbd[q].astype(BF16) for q in Q]
        pw = [_dot(n16[q], n16[q]) for q in Q]
        for lev in range(1, levels):
            p16 = [pw[q].astype(BF16) for q in Q]
            if lev + 1 < levels:
                pt = [_dot(p16[q], jnp.concatenate([p16[q], tinv[q].astype(BF16)], axis=1)) for q in Q]
                pw = [pt[q][:, :LANES] for q in Q]
                tinv = [tinv[q] + pt[q][:, LANES:] for q in Q]
            else:
                tinv = [tinv[q] + _dot(p16[q], tinv[q].astype(BF16)) for q in Q]
    ust = [_dot(tinv[q].astype(BF16), stack(rhs[q]).astype(BF16)) for q in Q]
    u = [ust[q][:C] + ust[q][C:] for q in Q]
    y = [xs_r[q] + _dot(jnp.concatenate([a_rb[q], a_rk[q]], axis=1).astype(BF16),
                        jnp.concatenate([ust[q].astype(BF16), vst[q]], axis=0)) for q in Q]
    uvt = [[_dot_nt(eye16, jnp.concatenate([seq_rows(u[q], s), seq_rows(v[q], s)], axis=0).astype(BF16))
            for s in range(nseq)] for q in Q]
    upd = [[_dot(uvt[q][s].astype(BF16),
                 jnp.concatenate([seq_rows(bt[q], s), seq_rows(kt[q], s)], axis=0).astype(BF16))
            for s in range(nseq)] for q in Q]
    for q in Q:
        for s in range(nseq):
            st_ref[s, q] = ((states[q][s] + jnp.where(bd, upd[q][s], 0.0))
                            * wcum[q][(s + 1) * Tc - 1:(s + 1) * Tc, :])
    for q in Q:
        sl = sls[q]
        mean = head_sum(y[q]) * (1.0 / HEAD_DIM)
        d = y[q] - mean
        var = head_sum(d * d) * (1.0 / HEAD_DIM)
        yn = d * lax.rsqrt(var + GN_EPS) * lnw_ref[:, sl] + lnb_ref[:, sl]
        bonus = head_sum(r[q] * km[q] * rk_ref[:, sl]) * v[q]
        y_ref[:, sl] = (yn + bonus) * g_ref[:, sl]

    @pl.when(pl.program_id(2) == pl.num_programs(2) - 1)
    def _():
        for s in range(nseq):
            for q in Q:
                full = st_ref[s, q]
                so_ref[s, 2 * q] = full[:HEAD_DIM, :HEAD_DIM]
                so_ref[s, 2 * q + 1] = full[HEAD_DIM:, HEAD_DIM:]


def _rwkv_scan(l, L, r, k, v, lw, a, g, k_k, k_a, r_k, ln_w, ln_b, s0, states_all, nstates, Tseq, Tc, G=16):
    M, D = r.shape
    C = SCAN_ROWS
    nseq = C // Tc
    nc = Tseq // Tc
    assert 2 * C == LANES and (nc == 1 or nseq == 1)
    G = min(G, D // LANES)
    H = D // HEAD_DIM
    tok = pl.BlockSpec((C, G * LANES), lambda b, p, c: (b * nc + c, p))
    par = _lspec(l, (1, G * LANES), lambda b, p, c: (0, p))
    st_blk = _lspec(l, (nseq, 2 * G, HEAD_DIM, HEAD_DIM), lambda b, p, c: (b, p, 0, 0))
    args = [r, k, v, lw, a, g, k_k, k_a, r_k, ln_w, ln_b]
    specs = [tok] * 6 + [par] * 5
    if s0 is not None:
        args.append(s0)
        specs.append(st_blk)
    aliases = {}
    if states_all is not None:
        aliases[len(args)] = 1
        args.append(states_all)
        specs.append(pl.BlockSpec(memory_space=pl.ANY))
    return pl.pallas_call(
        functools.partial(_scan_kernel, G=G, Tc=Tc, zero_init=s0 is None),
        grid=(M // (C * nc), D // (G * LANES), nc),
        in_specs=specs,
        out_specs=[tok, st_blk],
        out_shape=[jax.ShapeDtypeStruct((M, D), F32),
                   jax.ShapeDtypeStruct((L, nstates, H, HEAD_DIM, HEAD_DIM), F32)],
        scratch_shapes=[pltpu.VMEM((nseq, G, LANES, LANES), F32)],
        input_output_aliases=aliases,
        compiler_params=_cp("parallel", "parallel", "arbitrary"),
        name="rwkv_scan",
    )(*args)


def _pool_kernel(*refs, has_halo, seq_len, group_rows, out_row0, pos0, gd):
    if has_halo:
        p_ref, h_ref, pw_ref, ps_ref, o_ref = refs
    else:
        p_ref, pw_ref, ps_ref, o_ref = refs
    tm = p_ref.shape[0]
    if has_halo:
        base = (pl.program_id(0) * tm) % seq_len
        halo = jnp.where(base == 0, 0.0, h_ref[...])
        ext = jnp.concatenate([halo, p_ref[...]], axis=0)
        off = h_ref.shape[0]
        row = lax.broadcasted_iota(jnp.int32, (tm, 1), 0)
        pos = base + row + pos0
    else:
        ext = p_ref[...]
        off = 0
        row = lax.broadcasted_iota(jnp.int32, (tm, 1), 0)
        pos = pos0 + (row % group_rows) - out_row0
    for gi, w in enumerate(POOL_WINDOWS):
        x = ext[:, gi * gd:(gi + 1) * gd]
        s = x
        span = 1
        while span < w:
            s = s + pltpu.roll(s, span, 0)
            span *= 2
        cnt = jnp.minimum(w, pos + 1).astype(F32)
        mixed = s[off:] / cnt - x[off:]
        y = _dot(mixed.astype(BF16), pw_ref[gi])
        o_ref[:, gi * gd:(gi + 1) * gd] = y * ps_ref[:, gi * gd:(gi + 1) * gd]


def _pool(P, l, pcol, D, pw16, ps, *, seq_len, tm=256):
    M = P.shape[0]
    tm = min(tm, seq_len)
    G = len(POOL_WINDOWS)
    gd = D // G
    return pl.pallas_call(
        functools.partial(_pool_kernel, has_halo=True, seq_len=seq_len, group_rows=1, out_row0=0,
                          pos0=0, gd=gd),
        grid=(M // tm,),
        in_specs=[pl.BlockSpec((tm, D), lambda i: (i, pcol)),
                  pl.BlockSpec((16, D), lambda i: (jnp.maximum(i * (tm // 16) - 1, 0), pcol)),
                  _lspec(l, (G, gd, gd), lambda i: (0, 0, 0)),
                  _lspec(l, (1, D), lambda i: (0, 0))],
        out_specs=pl.BlockSpec((tm, D), lambda i: (i, 0)),
        out_shape=jax.ShapeDtypeStruct((M, D), F32),
        compiler_params=_cp("parallel"),
        name="pool",
    )(P, P, pw16, ps)


def _pool_ext(ext, l, D, pw16, ps, *, group_rows, out_row0, pos0, tm=384):
    M = ext.shape[0]
    tm = min(tm, M)
    G = len(POOL_WINDOWS)
    gd = D // G
    return pl.pallas_call(
        functools.partial(_pool_kernel, has_halo=False, seq_len=1, group_rows=group_rows,
                          out_row0=out_row0, pos0=pos0, gd=gd),
        grid=(M // tm,),
        in_specs=[pl.BlockSpec((tm, D), lambda i: (i, 0)),
                  _lspec(l, (G, gd, gd), lambda i: (0, 0, 0)),
                  _lspec(l, (1, D), lambda i: (0, 0))],
        out_specs=pl.BlockSpec((tm, D), lambda i: (i, 0)),
        out_shape=jax.ShapeDtypeStruct((M, D), F32),
        compiler_params=_cp("parallel"),
        name="pool_ext",
    )(ext, pw16, ps)


def _merge_out_kernel(ga_ref, gb_ref, ya_ref, yb_ref, w_ref, o_ref):
    m = jax.nn.sigmoid(ga_ref[...]) * ya_ref[...] + jax.nn.sigmoid(gb_ref[...]) * yb_ref[...]
    o_ref[...] = _dot(m.astype(BF16), w_ref[...])


def _merge_out(P, l, D, ya, yb, wout16, tm=256):
    M = P.shape[0]
    tm = min(tm, M)
    tok = pl.BlockSpec((tm, D), lambda i: (i, 0))
    return pl.pallas_call(
        _merge_out_kernel,
        grid=(M // tm,),
        in_specs=[pl.BlockSpec((tm, D), lambda i: (i, 4)),
                  pl.BlockSpec((tm, D), lambda i: (i, 5)),
                  tok, tok,
                  _lspec(l, (D, D), lambda i: (0, 0))],
        out_specs=tok,
        out_shape=jax.ShapeDtypeStruct((M, D), F32),
        compiler_params=_cp("parallel"),
        name="merge_out",
    )(P, P, ya, yb, wout16)


PEER_RANKS = PEER_TOPK + 1
PEER_RANK_ROWS = _rup(PEER_RANKS, 8)
PEER_PAIR_COUNTS = tuple(min(PEER_RANKS, PEER_RANKS // (i + 1)) for i in range(PEER_RANKS))
PEER_CAND_ROWS = _rup(sum(PEER_PAIR_COUNTS), 8)


def _peer_select_kernel(qt_ref, keys_ref, ethr_ref, e1_ref, e2_ref,
                        s1_scr, s2_ref, t1_scr, t2_scr, cand_scr, top_scr, *, heads):
    half = N_KEYS
    tl = qt_ref.shape[1]
    neg = -jnp.inf
    for h in range(heads):
        q1 = qt_ref[2 * h * half:(2 * h + 1) * half, :].astype(BF16)
        q2 = qt_ref[(2 * h + 1) * half:(2 * h + 2) * half, :].astype(BF16)
        s1_scr[h] = _dot(keys_ref[0], q1)
        s2_ref[h] = _dot(keys_ref[1], q2)

    def next_max(x, prev):
        return jnp.max(jnp.where(x < prev, x, neg), axis=0, keepdims=True)

    def rank_body(i, prev):
        new = []
        for h in range(heads):
            m1 = next_max(s1_scr[h], prev[2 * h])
            m2 = next_max(s2_ref[h], prev[2 * h + 1])
            t1_scr[h, pl.ds(i, 1), :] = m1
            t2_scr[h, pl.ds(i, 1), :] = m2
            new += [m1, m2]
        return tuple(new)

    inf_row = jnp.full((1, tl), jnp.inf, F32)
    lax.fori_loop(0, PEER_RANKS, rank_body, (inf_row,) * (2 * heads))

    for h in range(heads):
        cand_scr[h, PEER_CAND_ROWS - 8:, :] = jnp.full((8, tl), neg, F32)
        r0 = 0
        for i, n in enumerate(PEER_PAIR_COUNTS):
            cand_scr[h, r0:r0 + n, :] = t1_scr[h, i:i + 1, :] + t2_scr[h, 0:n, :]
            r0 += n

    def cand_body(i, prev):
        new = []
        for h in range(heads):
            m = next_max(cand_scr[h], prev[h])
            top_scr[h, pl.ds(i, 1), :] = m
            new.append(m)
        return tuple(new)

    lax.fori_loop(0, PEER_RANKS, cand_body, (inf_row,) * heads)

    for h in range(heads):
        top = top_scr[h]
        t16 = top[PEER_TOPK - 1:PEER_TOPK]
        t17 = top[PEER_TOPK:PEER_TOPK + 1]
        mid = jnp.where(t17 > neg, 0.5 * (t16 + t17), t16)
        z = jnp.sum(jnp.exp(top[:PEER_TOPK] - top[0:1]), axis=0, keepdims=True)
        s1 = s1_scr[h]
        m2 = t2_scr[h, 0:1, :]
        ethr_ref[h] = jnp.exp((mid - m2) - s1)
        e1_ref[h] = jnp.exp(s1 - t1_scr[h, 0:1, :]) / z
        e2_ref[h] = jnp.exp(s2_ref[h] - m2)


def _peer_select(qt, keys16, l, heads, tl=128):
    QD, M = qt.shape
    tl = min(tl, M)
    big = jax.ShapeDtypeStruct((heads, N_KEYS, M), F32)
    bspec = pl.BlockSpec((heads, N_KEYS, tl), lambda i: (0, 0, i))
    ranks = pltpu.VMEM((heads, PEER_RANK_ROWS, tl), F32)
    return pl.pallas_call(
        functools.partial(_peer_select_kernel, heads=heads),
        grid=(M // tl,),
        in_specs=[pl.BlockSpec((QD, tl), lambda i: (0, i)),
                  _lspec(l, keys16.shape[1:], lambda i: (0, 0, 0))],
        out_specs=[bspec] * 3,
        out_shape=[big] * 3,
        scratch_shapes=[pltpu.VMEM((heads, N_KEYS, tl), F32), pltpu.VMEM((heads, N_KEYS, tl), F32), ranks, ranks,
                        pltpu.VMEM((heads, PEER_CAND_ROWS, tl), F32), ranks],
        compiler_params=_cp("parallel"),
        name="peer_select",
    )(qt, keys16)


PEER_LANE_TILE = 128
PEER_STEP_KEYS = 4
PEER_UNIT_KEYS = 1
PEER_UNIT_GROUPS = 16


def _gelu_tanh(x):
    return 0.5 * x * (1.0 + jnp.tanh(0.7978845608028654 * (x + 0.044715 * (x * x * x))))


def _peer_dense_kernel(zt_ref, u_ref, vt_ref, thr_ref, e1_ref, e2_ref, o_ref, ht0, ht1, wt0, wt1, *, heads, nblk):
    s = pl.program_id(1)
    tm = zt_ref.shape[1]
    lt = min(PEER_LANE_TILE, tm)
    ka = PEER_STEP_KEYS
    uk = PEER_UNIT_KEYS
    groups = N_KEYS // 8
    mxu_n = min(MXU_WIDTH, tm)
    n_lane = tm // mxu_n
    d_out = o_ref.shape[0]
    ug = PEER_UNIT_GROUPS
    units = [(t0, ap, g0) for t0 in range(0, tm, lt) for ap in range(0, ka, uk) for g0 in range(0, groups, ug)]
    nu = len(units)

    @pl.when(s == 0)
    def _():
        o_ref[...] = jnp.zeros_like(o_ref)
        for buf in (ht0, ht1, wt0, wt1):
            buf[...] = jnp.zeros_like(buf)

    valid = jnp.logical_and(s >= 1, s <= nblk)

    def order_token(res):
        bits = pltpu.bitcast(res[:8, :lt], jnp.int32)
        bits = lax.shift_right_logical(lax.shift_right_logical(bits, 16), 16)
        return bits.astype(F32)

    def body(ht_new, ht_old, wt_new, wt_old, k0):
        n_sc = 2 * n_lane
        sc_rows = ka * N_KEYS // 2
        n_cb = 4 * n_lane
        ob = d_out // (n_cb // n_lane)

        def scores(piece):
            rs = slice((piece // n_lane) * sc_rows, (piece // n_lane + 1) * sc_rows)
            ts = slice((piece % n_lane) * mxu_n, (piece % n_lane + 1) * mxu_n)
            res = _dot(u_ref[rs, :], zt_ref[:, ts])
            ht_new[rs, ts] = res
            return order_token(res)

        def combine(piece):
            os_ = slice((piece // n_lane) * ob, (piece // n_lane + 1) * ob)
            ts = slice((piece % n_lane) * mxu_n, (piece % n_lane + 1) * mxu_n)
            res = _dot(vt_ref[os_, :], wt_new[:, ts])
            o_ref[os_, ts] += res
            return order_token(res)

        def weights(unit, start):
            t0, ap, g0 = units[unit]
            ts = slice(t0, t0 + lt)
            accs = [[start for _ in range(ug)] for _ in range(uk)]
            for h in range(heads):
                thr_b = [jnp.broadcast_to(thr_ref[h, k0 + ap + d:k0 + ap + d + 1, ts], (8, lt)) for d in range(uk)]
                e1_b = [jnp.broadcast_to(e1_ref[h, k0 + ap + d:k0 + ap + d + 1, ts], (8, lt)) for d in range(uk)]
                for g in range(ug):
                    e2t = e2_ref[h, (g0 + g) * 8:(g0 + g + 1) * 8, ts]
                    for d in range(uk):
                        accs[d][g] = accs[d][g] + jnp.where(e2t >= thr_b[d], e1_b[d] * e2t, 0.0)
            for d in range(uk):
                for g in range(0, ug, 2):
                    rs = slice((ap + d) * N_KEYS + (g0 + g) * 8, (ap + d) * N_KEYS + (g0 + g + 2) * 8)
                    w = jnp.concatenate([accs[d][g], accs[d][g + 1]], axis=0) * _gelu_tanh(ht_old[rs, ts])
                    wt_old[rs, ts] = jnp.where(valid, w, 0.0).astype(BF16)

        zero = jnp.zeros((8, lt), F32)
        pending, ready = [], []
        sc_every, cb_every = max(nu // n_sc, 1), max(nu // n_cb, 1)
        for unit in range(nu):
            if unit % cb_every == 0 and unit // cb_every < n_cb:
                pending.append(combine(unit // cb_every))
            if unit % sc_every == 0 and unit // sc_every < n_sc:
                pending.append(scores(unit // sc_every))
            start = zero
            for tok in ready:
                start = start + tok
            ready, pending = pending, []
            weights(unit, start)

    @pl.when(s % 2 == 0)
    def _():
        body(ht0, ht1, wt0, wt1, (8 - ka) % 8)

    @pl.when(s % 2 == 1)
    def _():
        body(ht1, ht0, wt1, wt0, 0)


def _peer_dense(zt, u16, vt16, l, ethr, e1, e2, tm=1024):
    D, M = zt.shape
    E = u16.shape[1]
    heads = e2.shape[0]
    tm = min(tm, M)
    ka = PEER_STEP_KEYS
    ec = ka * N_KEYS
    nblk = E // ec
    blk = lambda s_, back: jnp.clip(s_ - back, 0, nblk - 1)
    assert ka in (4, 8)
    rows_prev = pl.BlockSpec((heads, 8, tm), lambda i, s_: (0, blk(s_, 1) * ka // 8, i))
    buf_f32 = pltpu.VMEM((ec, tm), F32)
    buf_b16 = pltpu.VMEM((ec, tm), BF16)
    return pl.pallas_call(
        functools.partial(_peer_dense_kernel, heads=heads, nblk=nblk),
        grid=(M // tm, nblk + 2),
        in_specs=[pl.BlockSpec((D, tm), lambda i, s_: (0, i)),
                  _lspec(l, (ec, D), lambda i, s_: (blk(s_, 0), 0)),
                  _lspec(l, (D, ec), lambda i, s_: (0, blk(s_, 2))),
                  rows_prev, rows_prev,
                  pl.BlockSpec((heads, N_KEYS, tm), lambda i, s_: (0, 0, i))],
        out_specs=pl.BlockSpec((D, tm), lambda i, s_: (0, i)),
        out_shape=jax.ShapeDtypeStruct((D, M), F32),
        scratch_shapes=[buf_f32, buf_f32, buf_b16, buf_b16],
        compiler_params=_cp("parallel", "arbitrary"),
        name="peer_dense",
    )(zt, u16, vt16, ethr, e1, e2)


def _pad_last(w, n):
    return jnp.pad(w, [(0, 0)] * (w.ndim - 1) + [(0, n - w.shape[-1])])


def _pad_rows(w, n):
    return jnp.pad(w, [(0, 0)] * (w.ndim - 2) + [(0, n - w.shape[-2]), (0, 0)])


def kernel(x_prompt, x_sample, c_prompt, c_sample, state_wkv, state_shift, state_pool, norm1_g, norm2_g, final_g, w_mod, b_mod, w_in, mu_shift, w0, w_up, a0, a_up, g_up, k_k, k_a, r_k, ln_w, ln_b, pool_w, pool_scale, w_out, w_q, sub_keys, peer_u, peer_v):
    Bp, Tp, D = x_prompt.shape
    Bs, Ts, _ = x_sample.shape
    L = w_in.shape[0]
    H = D // HEAD_DIM
    dw, da, dg = w_up.shape[1], a_up.shape[1], g_up.shape[1]
    LW, LA, LG = _rup(dw, LANES), _rup(da, LANES), _rup(dg, LANES)
    LB = LW + LA + LG
    o3 = 3 * D
    o4, o5, o6 = o3 + dw, o3 + dw + da, o3 + dw + da + dg
    peer_heads = w_q.shape[2] // (2 * N_KEYS)
    Mp, Ms = Bp * Tp, Bs * Ts
    TS8 = 8
    PG = 24
    assert Tp % SCAN_ROWS == 0 and Ts <= TS8 and POOL_BUF + Ts <= PG

    Mc = _rup(Bp + Bs, 8)
    c_all = jnp.pad(jnp.concatenate([c_prompt, c_sample], axis=0), ((0, Mc - Bp - Bs), (0, 0)))
    mod_all = _modulation(c_all, w_mod, b_mod)
    tmn, tms = min(256, Tp), min(256, Ms)
    bpg = Tp // tmn
    mod_p = mod_all[:, :Bp].reshape(L, Bp, 1, 6 * D)
    mod_s = jnp.repeat(mod_all[:, Bp:Bp + Bs], Ts, axis=1).reshape(L, Ms // tms, tms, 6 * D)

    def repack_cols(t):
        return t[..., :o3], jnp.concatenate(
            [_pad_last(t[..., o3:o4], LW), _pad_last(t[..., o4:o5], LA), _pad_last(t[..., o5:o6], LG)], axis=-1)

    w_rkv, w_l = repack_cols(w_in)
    winp = jnp.concatenate([w_rkv, w_in[..., o6:], w_l], axis=-1).astype(BF16)
    mu_rkv, mu_l = repack_cols(mu_shift[:, None, :])
    wup = _pad_rows(w_up, LW).astype(BF16)
    aup = _pad_rows(a_up, LA).astype(BF16)
    gup = _pad_rows(g_up, LG).astype(BF16)
    rows = lambda t: t.reshape(L, 1, D)
    w0r, a0r, kkr, kar, rkr, lnwr, lnbr, psr = (rows(t) for t in (w0, a0, k_k, k_a, r_k, ln_w, ln_b, pool_scale))
    n1r, n2r = rows(norm1_g), rows(norm2_g)
    pw16 = pool_w.astype(BF16)
    wout16 = w_out.astype(BF16)
    wqt16 = jnp.swapaxes(w_q, 1, 2).astype(BF16)
    keys16 = sub_keys.astype(BF16)
    u16 = peer_u.astype(BF16)
    vt16 = jnp.swapaxes(peer_v, 1, 2).astype(BF16)
    gain = lambda arr, l: (arr, _lspec(l, (1, D), lambda i: (0, 0)))
    st_rkv, st_l = repack_cols(state_shift)
    expand = lambda t: jnp.pad(t[:, :, None, :], ((0, 0), (0, 0), (0, Ts - 1), (0, 0))).reshape(L, Ms, -1)
    st_rkv, st_l = expand(st_rkv), expand(st_l)

    def shift_out(P, nseq, T):
        last = P.reshape(nseq, T, -1)[:, -1]
        return jnp.concatenate(
            [last[:, :o3], last[:, 6 * D:6 * D + dw], last[:, 6 * D + LW:6 * D + LW + da],
             last[:, 6 * D + LW + LA:6 * D + LW + LA + dg]], axis=-1)

    xp = x_prompt.reshape(Mp, D)
    xs = x_sample.reshape(Ms, D)
    yp = ys = None
    outs = {k_: [] for k_ in ("shift_p", "pool_p", "shift_s", "pool_s")}
    wkv_p = wkv_s = None

    for l in range(L):
        def half_layer_1(x, y_prev, mod4, bpg_, tm_):
            res = None if y_prev is None else (y_prev, mod4, l - 1, 5, bpg_)
            x_new, z = _resnorm(x, gain(n1r, l), res=res, mod=(mod4, l, 1, 0, bpg_), tm=tm_, y_t=True)
            return (x if res is None else x_new), _mm(z, winp, lb=l, tn=_lane_tile(winp.shape[-1], 1280))

        def peer(zt):
            qt = _mm(wqt16, zt, la=l, tm=wqt16.shape[1], tn=512)
            ethr, e1, e2 = _peer_select(qt, keys16, l, peer_heads)
            return _peer_dense(zt, u16, vt16, l, ethr, e1, e2)

        scan_par = (kkr, kar, rkr, lnwr, lnbr)

        xp, P = half_layer_1(xp, yp, mod_p, bpg, tmn)
        r, k, v, lw, a, g = _rwkv_prep(P, l, D, LB, LW, LA, Tp, None, mu_rkv, mu_l, w0r, a0r, wup, aup, gup)
        ya, wkv_p = _rwkv_scan(l, L, r, k, v, lw, a, g, *scan_par, None, wkv_p, Bp, Tp, SCAN_ROWS)
        yb = _pool(P, l, 3, D, pw16, psr, seq_len=Tp)
        y1 = _merge_out(P, l, D, ya, yb, wout16)
        xp, z2 = _resnorm(xp, gain(n2r, l), res=(y1, mod_p, l, 2, bpg), mod=(mod_p, l, 4, 3, bpg), tm=tmn,
                          z_t=True)
        yp = peer(z2)
        outs["shift_p"].append(shift_out(P, Bp, Tp))
        outs["pool_p"].append(P.reshape(Bp, Tp, -1)[:, Tp - POOL_BUF:, o3:o3 + D])

        xs, P = half_layer_1(xs, ys, mod_s, 1, tms)
        r, k, v, lw, a, g = _rwkv_prep(P, l, D, LB, LW, LA, Ts, (st_rkv, st_l), mu_rkv, mu_l,
                                       w0r, a0r, wup, aup, gup)
        pad8 = lambda t: jnp.pad(t.reshape(Bs, Ts, D), ((0, 0), (0, TS8 - Ts), (0, 0))).reshape(Bs * TS8, D)
        ya8, wkv_s = _rwkv_scan(l, L, pad8(r), pad8(k), pad8(v), pad8(lw), pad8(a), pad8(g), *scan_par,
                                state_wkv, wkv_s, Bs, TS8, TS8, G=8)
        ya = ya8.reshape(Bs, TS8, D)[:, :Ts].reshape(Ms, D)
        p_new = P[:, o3:o3 + D].reshape(Bs, Ts, D)
        ext = jnp.concatenate([state_pool[l], p_new], axis=1)
        extp = jnp.pad(ext, ((0, 0), (PG - POOL_BUF - Ts, 0), (0, 0))).reshape(Bs * PG, D)
        yb = _pool_ext(extp, l, D, pw16, psr, group_rows=PG, out_row0=PG - Ts, pos0=PAST_LEN)
        yb = yb.reshape(Bs, PG, D)[:, PG - Ts:].reshape(Ms, D)
        y1 = _merge_out(P, l, D, ya, yb, wout16)
        xs, z2 = _resnorm(xs, gain(n2r, l), res=(y1, mod_s, l, 2, 1), mod=(mod_s, l, 4, 3, 1), tm=tms,
                          z_t=True)
        ys = peer(z2)
        outs["shift_s"].append(shift_out(P, Bs, Ts))
        outs["pool_s"].append(ext[:, -POOL_BUF:])

    fin = (final_g.reshape(1, D), pl.BlockSpec((1, D), lambda i: (0, 0)))
    _, y_prompt = _resnorm(xp, fin, res=(yp, mod_p, L - 1, 5, bpg), z_dtype=F32, tm=tmn, y_t=True)
    _, y_sample = _resnorm(xs, fin, res=(ys, mod_s, L - 1, 5, 1), z_dtype=F32, tm=tms, y_t=True)
    st = lambda name: jnp.stack(outs[name], axis=0)
    return (y_prompt.reshape(Bp, Tp, D), y_sample.reshape(Bs, Ts, D),
            wkv_p, st("shift_p"), st("pool_p"), wkv_s, st("shift_s"), st("pool_s"))
```
